```python
import math
import jax, jax.numpy as jnp
from jax import lax
import numpy as np

D_MODEL = 1024
BATCH = 4
SEQ = 4096
DEPTH = 4
DEC_BATCH = 32
DEC_SEQ = 1
PAST_LEN = 8192
PAGE_SIZE = 128

N_MIXERS = 2
N_ATTN_LAYERS = (DEPTH + 1) // 2
N_SSM_LAYERS = DEPTH // 2

ATTN_GROUPS = ((128, 1), (512, 4), (2048, 16))
N_ATTN_GROUPS = len(ATTN_GROUPS)
ATTN_HEADS = 16
HEAD_DIM = D_MODEL // ATTN_HEADS
ATTN_WIDTH = ATTN_HEADS * HEAD_DIM
ATTN_IN_WIDTH = N_ATTN_GROUPS * 3 * ATTN_WIDTH
ROPE_THETA = 10000.0
SCALE = HEAD_DIM ** -0.5

D_INNER = 2 * D_MODEL
SSM_HEAD_DIM = 64
SSM_HEADS = D_INNER // SSM_HEAD_DIM
SSM_GROUPS = 8
HEADS_PER_GROUP = SSM_HEADS // SSM_GROUPS
D_STATE = 128
CONV_K = 4
CONV_DIM = D_INNER + 2 * SSM_GROUPS * D_STATE
SSM_IN_WIDTH = D_INNER + CONV_DIM + SSM_HEADS
SSD_CHUNK = 128

D_FF = 4 * D_MODEL
ALPHA = (2 * DEPTH) ** 0.25
BETA = (8 * DEPTH) ** -0.25
LN_EPS = 1e-5
RMS_EPS = 1e-5

kernel_name = 'hybrid_dilated_attn_ssd_decoder_step'

F32 = jnp.float32


def layer_norm(x, g, b):
    xf = x.astype(F32)
    mu = jnp.mean(xf, -1, keepdims=True)
    xc = xf - mu
    var = jnp.mean(xc * xc, -1, keepdims=True)
    return (xc * lax.rsqrt(var + LN_EPS) * g + b).astype(x.dtype)


def rope(x, pos):
    half = HEAD_DIM // 2
    inv = jnp.power(ROPE_THETA, -jnp.arange(half, dtype=F32) * (2.0 / HEAD_DIM))
    ang = pos.astype(F32)[:, None] * inv[None, :]
    cos = jnp.cos(ang)[None, :, None, :]
    sin = jnp.sin(ang)[None, :, None, :]
    xf = x.astype(F32)
    x1, x2 = xf[..., :half], xf[..., half:]
    return jnp.concatenate([x1 * cos - x2 * sin, x2 * cos + x1 * sin], -1).astype(x.dtype)


def attn_project(x, w_in, pos):
    b, s, _ = x.shape
    qkv = (x @ w_in).reshape(b, s, N_ATTN_GROUPS, 3, ATTN_HEADS, HEAD_DIM)
    return [(rope(qkv[:, :, g, 0], pos), rope(qkv[:, :, g, 1], pos), qkv[:, :, g, 2])
            for g in range(N_ATTN_GROUPS)]


def attn_merge(outs, lses, w_out, dtype):
    wts = jax.nn.softmax(jnp.stack(lses, -1), axis=-1)
    o = jnp.einsum('gbshe,bshg->bshe', jnp.stack(outs, 0), wts)
    b, s = o.shape[:2]
    return o.astype(dtype).reshape(b, s, ATTN_WIDTH) @ w_out


def dilated_band_attention(q, k, v, dil, nstep):
    b, s, h, e = q.shape
    n = s // dil
    nb = -(-n // nstep)
    pad = nb * nstep - n

    def blocks(t):
        t = t.astype(F32).reshape(b, n, dil, h, e).transpose(0, 2, 1, 3, 4)
        t = jnp.pad(t, ((0, 0), (0, 0), (0, pad), (0, 0), (0, 0)))
        return t.reshape(b, dil, nb, nstep, h, e)

    def with_prev(t):
        prev = jnp.pad(t, ((0, 0), (0, 0), (1, 0), (0, 0), (0, 0), (0, 0)))[:, :, :nb]
        return jnp.concatenate([prev, t], axis=3)

    qb = blocks(q)
    kx = with_prev(blocks(k))
    vx = with_prev(blocks(v))
    scores = jnp.einsum('brnqhe,brnkhe->brnhqk', qb, kx) * SCALE
    qi = jnp.arange(nstep)[:, None]
    kj = jnp.arange(2 * nstep)[None, :]
    steps = qi + nstep - kj
    key_m = jnp.arange(nb)[:, None, None] * nstep - nstep + kj[None]
    mask = (steps >= 0) & (steps <= nstep) & (key_m >= 0)
    scores = jnp.where(mask[None, None, :, None], scores, -jnp.inf)
    mx = jnp.max(scores, -1, keepdims=True)
    p = jnp.exp(scores - mx)
    den = jnp.sum(p, -1)
    o = jnp.einsum('brnhqk,brnkhe->brnqhe', p, vx) / jnp.swapaxes(den, -1, -2)[..., None]
    lse = jnp.swapaxes(mx[..., 0] + jnp.log(den), -1, -2)
    o = o.reshape(b, dil, nb * nstep, h, e)[:, :, :n].transpose(0, 2, 1, 3, 4).reshape(b, s, h, e)
    lse = lse.reshape(b, dil, nb * nstep, h)[:, :, :n].transpose(0, 2, 1, 3).reshape(b, s, h)
    return o, lse


def dilated_gather_attention(q, k, v, buf, dil, nstep, window):
    t = q.shape[1]
    L = buf.shape[1]
    ext = jnp.concatenate([buf, jnp.stack([k, v], axis=2)], axis=1)
    idx = L + jnp.arange(t)[:, None] - dil * jnp.arange(nstep + 1)[None, :]
    valid = idx >= 0
    g = ext[:, jnp.maximum(idx, 0)]
    kg = g[:, :, :, 0].astype(F32)
    vg = g[:, :, :, 1].astype(F32)
    scores = jnp.einsum('bthe,btshe->bths', q.astype(F32), kg) * SCALE
    scores = jnp.where(valid[None, :, None, :], scores, -jnp.inf)
    mx = jnp.max(scores, -1, keepdims=True)
    p = jnp.exp(scores - mx)
    den = jnp.sum(p, -1)
    o = jnp.einsum('bths,btshe->bthe', p, vg) / den[..., None]
    lse = mx[..., 0] + jnp.log(den)
    new_buf = ext[:, -min(window, L + t):]
    return o, lse, new_buf


def attn_prompt(x, pos, w_in, w_out):
    qkv = attn_project(x, w_in, pos)
    outs, lses, bufs = [], [], []
    for (win, dil), (q, k, v) in zip(ATTN_GROUPS, qkv):
        o, l = dilated_band_attention(q, k, v, dil, win // dil)
        outs.append(o)
        lses.append(l)
        bufs.append(jnp.stack([k, v], axis=2)[:, -min(win, x.shape[1]):])
    return attn_merge(outs, lses, w_out, x.dtype), bufs


def attn_sample(x, pos, caches, w_in, w_out):
    qkv = attn_project(x, w_in, pos)
    outs, lses, bufs = [], [], []
    for (win, dil), (q, k, v), buf in zip(ATTN_GROUPS, qkv, caches):
        o, l, nbuf = dilated_gather_attention(q, k, v, buf, dil, win // dil, win)
        outs.append(o)
        lses.append(l)
        bufs.append(nbuf)
    return attn_merge(outs, lses, w_out, x.dtype), bufs


def ssd_chunked(x, dt, a, bm, cm, h0):
    b, s, g, r, p = x.shape
    n = bm.shape[-1]
    L = SSD_CHUNK if s % SSD_CHUNK == 0 else s
    nc = s // L
    xc = x.astype(F32).reshape(b, nc, L, g, r, p)
    dtc = dt.reshape(b, nc, L, g, r)
    bc = bm.astype(F32).reshape(b, nc, L, g, n)
    cc = cm.astype(F32).reshape(b, nc, L, g, n)
    acum = jnp.cumsum(dtc * a, axis=2)
    xdt = xc * dtc[..., None]
    at = jnp.moveaxis(acum, 2, -1)
    diff = at[..., :, None] - at[..., None, :]
    causal = jnp.tril(jnp.ones((L, L), dtype=bool))
    decay = jnp.exp(jnp.where(causal, diff, -jnp.inf))
    cb = jnp.einsum('bclgn,bcsgn->bcgls', cc, bc)
    y_diag = jnp.einsum('bcgrls,bcsgrp->bclgrp', cb[:, :, :, None] * decay, xdt)
    last = acum[:, :, -1]
    to_end = jnp.exp(last[:, :, None] - acum)
    states = jnp.einsum('bclgn,bclgrp->bcgrpn', bc, xdt * to_end[..., None])

    def step(h, inp):
        dec, st = inp
        return dec[..., None, None] * h + st, h

    h_final, h_in = lax.scan(step, h0.astype(F32),
                             (jnp.moveaxis(jnp.exp(last), 1, 0), jnp.moveaxis(states, 1, 0)))
    h_in = jnp.moveaxis(h_in, 0, 1)
    y_off = jnp.einsum('bclgn,bcgrpn->bclgrp', cc, h_in) * jnp.exp(acum)[..., None]
    return (y_diag + y_off).reshape(b, s, g, r, p), h_final


def ssd_mixer(x, conv_state, h0, w_in, conv_w, conv_b, dt_bias, a_log, d_skip, norm_w, w_out):
    b, s, _ = x.shape
    proj = x @ w_in
    z = proj[..., :D_INNER]
    xbc = proj[..., D_INNER:D_INNER + CONV_DIM]
    dt_raw = proj[..., D_INNER + CONV_DIM:]
    xbc_ext = jnp.concatenate([conv_state.astype(xbc.dtype), xbc], axis=1)
    new_conv = xbc_ext[:, -(CONV_K - 1):]
    xbc = lax.conv_general_dilated(xbc_ext, conv_w.astype(xbc.dtype)[:, None, :], (1,), 'VALID',
                                   dimension_numbers=('NWC', 'WIO', 'NWC'),
                                   feature_group_count=CONV_DIM)
    xbc = jax.nn.silu(xbc + conv_b)
    gn = SSM_GROUPS * D_STATE
    xs = xbc[..., :D_INNER].reshape(b, s, SSM_GROUPS, HEADS_PER_GROUP, SSM_HEAD_DIM)
    bm = xbc[..., D_INNER:D_INNER + gn].reshape(b, s, SSM_GROUPS, D_STATE)
    cm = xbc[..., D_INNER + gn:].reshape(b, s, SSM_GROUPS, D_STATE)
    dt = jax.nn.softplus(dt_raw.astype(F32) + dt_bias.astype(F32)).reshape(b, s, SSM_GROUPS, HEADS_PER_GROUP)
    a = -jnp.exp(a_log.astype(F32)).reshape(SSM_GROUPS, HEADS_PER_GROUP)
    y, h = ssd_chunked(xs, dt, a, bm, cm,
                       h0.reshape(b, SSM_GROUPS, HEADS_PER_GROUP, SSM_HEAD_DIM, D_STATE))
    y = y + xs.astype(F32) * d_skip.astype(F32).reshape(SSM_GROUPS, HEADS_PER_GROUP)[..., None]
    y = y.reshape(b, s, D_INNER) * jax.nn.silu(z.astype(F32))
    yg = y.reshape(b, s, SSM_GROUPS, D_INNER // SSM_GROUPS)
    yg = yg * lax.rsqrt(jnp.mean(yg * yg, -1, keepdims=True) + RMS_EPS)
    out = (yg.reshape(b, s, D_INNER) * norm_w).astype(x.dtype) @ w_out
    return out, new_conv, h.reshape(b, SSM_HEADS, SSM_HEAD_DIM, D_STATE)


def sq_relu_mlp(x, w1, w2):
    h = jax.nn.relu(x @ w1)
    return (h * h) @ w2


def setup_inputs(seed: int = 0) -> dict:
    key = jax.random.key(seed)
    ks = jax.random.split(key, 24)

    def nrm(k, shape, scale):
        return jax.random.normal(k, shape, F32) * scale

    kv_shape = lambda win: (N_ATTN_LAYERS, DEC_BATCH, min(win, PAST_LEN), 2, ATTN_HEADS, HEAD_DIM)
    dt0 = jnp.exp(jax.random.uniform(ks[12], (N_SSM_LAYERS, SSM_HEADS), F32,
                                     minval=math.log(1e-3), maxval=math.log(1e-1)))
    return {
        'x_prompt': nrm(ks[0], (BATCH, SEQ, D_MODEL), 1.0),
        'x_sample': nrm(ks[1], (DEC_BATCH, DEC_SEQ, D_MODEL), 1.0),
        'cache_kv_w128': nrm(ks[2], kv_shape(ATTN_GROUPS[0][0]), 1.0),
        'cache_kv_w512': nrm(ks[3], kv_shape(ATTN_GROUPS[1][0]), 1.0),
        'cache_kv_w2048': nrm(ks[4], kv_shape(ATTN_GROUPS[2][0]), 1.0),
        'state_ssm': nrm(ks[5], (N_SSM_LAYERS, DEC_BATCH, SSM_HEADS, SSM_HEAD_DIM, D_STATE), 0.1),
        'state_conv': nrm(ks[6], (N_SSM_LAYERS, DEC_BATCH, CONV_K - 1, CONV_DIM), 1.0),
        'attn_w_in': nrm(ks[7], (N_ATTN_LAYERS, D_MODEL, ATTN_IN_WIDTH), D_MODEL ** -0.5),
        'attn_w_out': nrm(ks[8], (N_ATTN_LAYERS, ATTN_WIDTH, D_MODEL), BETA * ATTN_WIDTH ** -0.5),
        'ssm_w_in': nrm(ks[9], (N_SSM_LAYERS, D_MODEL, SSM_IN_WIDTH), D_MODEL ** -0.5),
        'ssm_conv_w': nrm(ks[10], (N_SSM_LAYERS, CONV_K, CONV_DIM), CONV_K ** -0.5),
        'ssm_conv_b': nrm(ks[11], (N_SSM_LAYERS, CONV_DIM), 0.02),
        'ssm_dt_bias': dt0 + jnp.log(-jnp.expm1(-dt0)),
        'ssm_a_log': jnp.log(jax.random.uniform(ks[13], (N_SSM_LAYERS, SSM_HEADS), F32, minval=1.0, maxval=16.0)),
        'ssm_d': 1.0 + nrm(ks[14], (N_SSM_LAYERS, SSM_HEADS), 0.02),
        'ssm_norm_w': 1.0 + nrm(ks[15], (N_SSM_LAYERS, D_INNER), 0.02),
        'ssm_w_out': nrm(ks[16], (N_SSM_LAYERS, D_INNER, D_MODEL), BETA * D_INNER ** -0.5),
        'mlp_w1': nrm(ks[17], (DEPTH, D_MODEL, D_FF), D_MODEL ** -0.5),
        'mlp_w2': nrm(ks[18], (DEPTH, D_FF, D_MODEL), BETA * D_FF ** -0.5),
        'ln_mix_g': 1.0 + nrm(ks[19], (DEPTH, D_MODEL), 0.02),
        'ln_mix_b': nrm(ks[20], (DEPTH, D_MODEL), 0.02),
        'ln_ffn_g': 1.0 + nrm(ks[21], (DEPTH, D_MODEL), 0.02),
        'ln_ffn_b': nrm(ks[22], (DEPTH, D_MODEL), 0.02),
    }


def reference(x_prompt, x_sample, cache_kv_w128, cache_kv_w512, cache_kv_w2048, state_ssm, state_conv,
              attn_w_in, attn_w_out, ssm_w_in, ssm_conv_w, ssm_conv_b, ssm_dt_bias, ssm_a_log, ssm_d,
              ssm_norm_w, ssm_w_out, mlp_w1, mlp_w2, ln_mix_g, ln_mix_b, ln_ffn_g, ln_ffn_b):
    caches = (cache_kv_w128, cache_kv_w512, cache_kv_w2048)
    pos_p = jnp.arange(x_prompt.shape[1], dtype=jnp.int32)
    pos_s = PAST_LEN + jnp.arange(x_sample.shape[1], dtype=jnp.int32)
    xp, xs = x_prompt, x_sample
    bp, bs = xp.shape[0], xs.shape[0]
    kv_p = [[] for _ in ATTN_GROUPS]
    kv_s = [[] for _ in ATTN_GROUPS]
    ssm_p, conv_p, ssm_s, conv_s = [], [], [], []
    for i in range(DEPTH):
        j = i // N_MIXERS
        if i % N_MIXERS == 0:
            hp, bufs_p = attn_prompt(xp, pos_p, attn_w_in[j], attn_w_out[j])
            hs, bufs_s = attn_sample(xs, pos_s, [c[j] for c in caches], attn_w_in[j], attn_w_out[j])
            for g in range(N_ATTN_GROUPS):
                kv_p[g].append(bufs_p[g])
                kv_s[g].append(bufs_s[g])
        else:
            prm = (ssm_w_in[j], ssm_conv_w[j], ssm_conv_b[j], ssm_dt_bias[j], ssm_a_log[j],
                   ssm_d[j], ssm_norm_w[j], ssm_w_out[j])
            conv0 = jnp.zeros((bp, CONV_K - 1, CONV_DIM), xp.dtype)
            h0 = jnp.zeros((bp, SSM_HEADS, SSM_HEAD_DIM, D_STATE), F32)
            hp, cp, sp = ssd_mixer(xp, conv0, h0, *prm)
            hs, cs, ss = ssd_mixer(xs, state_conv[j], state_ssm[j], *prm)
            ssm_p.append(sp)
            conv_p.append(cp)
            ssm_s.append(ss)
            conv_s.append(cs)
        xp = layer_norm(ALPHA * xp + hp, ln_mix_g[i], ln_mix_b[i])
        xs = layer_norm(ALPHA * xs + hs, ln_mix_g[i], ln_mix_b[i])
        xp = layer_norm(ALPHA * xp + sq_relu_mlp(xp, mlp_w1[i], mlp_w2[i]), ln_ffn_g[i], ln_ffn_b[i])
        xs = layer_norm(ALPHA * xs + sq_relu_mlp(xs, mlp_w1[i], mlp_w2[i]), ln_ffn_g[i], ln_ffn_b[i])
    kv128_p, kv512_p, kv2048_p = jnp.stack(kv_p[0]), jnp.stack(kv_p[1]), jnp.stack(kv_p[2])
    kv128_s, kv512_s, kv2048_s = jnp.stack(kv_s[0]), jnp.stack(kv_s[1]), jnp.stack(kv_s[2])
    new_ssm_p, new_conv_p = jnp.stack(ssm_p), jnp.stack(conv_p)
    new_ssm_s, new_conv_s = jnp.stack(ssm_s), jnp.stack(conv_s)
    return (xp, xs, kv128_p, kv512_p, kv2048_p, new_ssm_p, new_conv_p,
            kv128_s, kv512_s, kv2048_s, new_ssm_s, new_conv_s)
```

```python
import functools
import math

import jax
import jax.numpy as jnp
from jax import lax
from jax.experimental import pallas as pl
from jax.experimental.pallas import tpu as pltpu

F32 = jnp.float32
BF16 = jnp.bfloat16

D_MODEL = 1024
DEPTH = 4
PAST_LEN = 8192
ATTN_GROUPS = ((128, 1), (512, 4), (2048, 16))
N_GROUPS = len(ATTN_GROUPS)
NSTEP = 128
ATTN_HEADS = 16
HEAD_DIM = 64
ATTN_WIDTH = ATTN_HEADS * HEAD_DIM
QKV_WIDTH = 3 * ATTN_WIDTH
ROPE_THETA = 10000.0
SCALE = HEAD_DIM ** -0.5

D_INNER = 2048
SSM_HEAD_DIM = 64
SSM_HEADS = 32
SSM_GROUPS = 8
HEADS_PER_GROUP = 4
GROUP_WIDTH = HEADS_PER_GROUP * SSM_HEAD_DIM
D_STATE = 128
CONV_K = 4
CONV_DIM = D_INNER + 2 * SSM_GROUPS * D_STATE
DT_COL = D_INNER + CONV_DIM
SSM_IN_PAD = DT_COL + 1024
SSD_CHUNK = 128

D_FF = 4096
ALPHA = (2 * DEPTH) ** 0.25
LN_EPS = 1e-5
RMS_EPS = 1e-5

LANES = 128
NEG = -1e30
VMEM_LIMIT = 56 * 1024 * 1024


def _cparams(sem):
    return pltpu.CompilerParams(dimension_semantics=sem, vmem_limit_bytes=VMEM_LIMIT)


def _dot(a, b):
    return jnp.dot(a, b, preferred_element_type=F32)


def _dot_nt(a, b):
    return lax.dot_general(a, b, (((1,), (1,)), ((), ())), preferred_element_type=F32)


def _dot_exact(x, e):
    hi = x.astype(BF16)
    r1 = x - hi.astype(F32)
    mid = r1.astype(BF16)
    lo = (r1 - mid.astype(F32)).astype(BF16)
    return _dot(hi, e) + _dot(mid, e) + _dot(lo, e)


def _silu(x):
    return x / (1.0 + jnp.exp(-x))


def _softplus(x):
    return jnp.maximum(x, 0.0) + jnp.log(1.0 + jnp.exp(-jnp.abs(x)))


def _layer_norm(y, g, b):
    mu = jnp.mean(y, axis=-1, keepdims=True)
    yc = y - mu
    var = jnp.mean(yc * yc, axis=-1, keepdims=True)
    return yc * lax.rsqrt(var + LN_EPS) * g + b


def _proj_kernel(x_ref, w_ref, o_ref):
    o_ref[...] = _dot(x_ref[...].astype(BF16), w_ref[...])


def _proj_rope_kernel(x_ref, w_ref, cos_ref, sa_ref, sb_ref, o_ref):
    kind = pl.program_id(0) % 3
    acc = _dot(x_ref[...].astype(BF16), w_ref[...])

    @pl.when(kind == 2)
    def _():
        o_ref[...] = acc

    @pl.when(kind != 2)
    def _():
        sc = jnp.where(kind == 0, SCALE, 1.0).astype(F32)
        cos = cos_ref[...] * sc
        sa = sa_ref[...] * sc
        sb = sb_ref[...] * sc
        for c in range(ATTN_WIDTH // LANES):
            seg = acc[:, c * LANES:(c + 1) * LANES]
            o_ref[:, c * LANES:(c + 1) * LANES] = (
                seg * cos + pltpu.roll(seg, 96, 1) * sa + pltpu.roll(seg, 32, 1) * sb)


def _project(x, w, tm, tn, rope=None, table_blocks=1):
    m, k = x.shape
    n = w.shape[1]
    grid = (n // tn, m // tm)
    in_specs = [pl.BlockSpec((tm, k), lambda j, i: (i, 0)),
                pl.BlockSpec((k, tn), lambda j, i: (0, j))]
    args = [x, w]
    if rope is None:
        body = _proj_kernel
    else:
        body = _proj_rope_kernel
        tspec = pl.BlockSpec((tm, LANES), lambda j, i: (i % table_blocks, 0))
        in_specs += [tspec, tspec, tspec]
        args += list(rope)
    return pl.pallas_call(
        body,
        grid=grid,
        in_specs=in_specs,
        out_specs=pl.BlockSpec((tm, tn), lambda j, i: (i, j)),
        out_shape=jax.ShapeDtypeStruct((m, n), F32),
        compiler_params=_cparams(("arbitrary", "arbitrary")),
        name="proj_rope" if rope is not None else "proj",
    )(*args)


def _rope_tables(pos):
    half = HEAD_DIM // 2
    inv = jnp.power(ROPE_THETA, -jnp.arange(half, dtype=F32) * (2.0 / HEAD_DIM))
    ang = pos.astype(F32)[:, None] * inv[None, :]
    cos = jnp.tile(jnp.cos(ang), (1, LANES // half))
    sin = jnp.tile(jnp.sin(ang), (1, LANES // half))
    first = (jnp.arange(LANES) % HEAD_DIM) < half
    sa = jnp.where(first[None, :], -sin, 0.0)
    sb = jnp.where(first[None, :], 0.0, sin)
    return cos, sa, sb


def _attn_kernel(q_ref, k_ref, v_ref, o_ref, m_s, l_s, acc_s, *, seq):
    g = pl.program_id(2)
    lane = lax.broadcasted_iota(jnp.int32, (NSTEP, LANES), 1)
    lane_a = lane < HEAD_DIM
    dmat = (lax.broadcasted_iota(jnp.int32, (NSTEP, 2 * NSTEP), 1)
            - lax.broadcasted_iota(jnp.int32, (NSTEP, 2 * NSTEP), 0))

    def head(qh, kw, vw, off):
        s = _dot_nt(qh, kw)
        s = jnp.where(dmat <= off, jnp.where(dmat >= off - NSTEP, s, NEG), NEG)
        mx = jnp.max(s, axis=1, keepdims=True)
        p = jnp.exp(s - mx)
        den = jnp.sum(p, axis=1, keepdims=True)
        return mx, den, _dot(p.astype(BF16), vw)

    def run_group(dil, first):
        nb = seq // dil // NSTEP

        def rows(start, size):
            if dil == 1:
                return pl.ds(pl.multiple_of(start, NSTEP), size)
            return pl.ds(start, size, stride=dil)

        def block(r, n):
            q_rows = rows(r + dil * NSTEP * n, NSTEP)
            w_rows = rows(r + dil * NSTEP * jnp.maximum(n - 1, 0), 2 * NSTEP)
            off = jnp.where(n > 0, NSTEP, 0)
            qb = q_ref[q_rows, :].astype(BF16)
            kw = k_ref[w_rows, :].astype(BF16)
            vw = v_ref[w_rows, :].astype(BF16)
            zero = jnp.zeros_like(qb)
            m_a, l_a, o_a = head(jnp.where(lane_a, qb, zero), kw, vw, off)
            m_b, l_b, o_b = head(jnp.where(lane_a, zero, qb), kw, vw, off)
            m_blk = jnp.where(lane_a, m_a, m_b)
            l_blk = jnp.where(lane_a, l_a, l_b)
            o_blk = jnp.where(lane_a, o_a, o_b)
            if first:
                m_s[q_rows, :] = m_blk
                l_s[q_rows, :] = l_blk
                acc_s[q_rows, :] = o_blk
            else:
                m_old = m_s[q_rows, :]
                m_new = jnp.maximum(m_old, m_blk)
                a_old = jnp.exp(m_old - m_new)
                a_blk = jnp.exp(m_blk - m_new)
                m_s[q_rows, :] = m_new
                l_s[q_rows, :] = a_old * l_s[q_rows, :] + a_blk * l_blk
                acc_s[q_rows, :] = a_old * acc_s[q_rows, :] + a_blk * o_blk

        def res_body(r, c):
            def blk_body(n, c2):
                block(r, n)
                return c2
            return lax.fori_loop(0, nb, blk_body, c)

        lax.fori_loop(0, dil, res_body, 0)

    for gi, (_, dil) in enumerate(ATTN_GROUPS):
        @pl.when(g == gi)
        def _(dil=dil, gi=gi):
            run_group(dil, gi == 0)

    @pl.when(g == N_GROUPS - 1)
    def _():
        o_ref[...] = (acc_s[...] / l_s[...]).astype(o_ref.dtype)


def _attention_prompt(qkv, batch, seq):
    m = qkv.shape[0]
    cpg = QKV_WIDTH // LANES
    cpk = ATTN_WIDTH // LANES
    blk = (seq, LANES)
    return pl.pallas_call(
        functools.partial(_attn_kernel, seq=seq),
        grid=(batch, cpk, N_GROUPS),
        in_specs=[pl.BlockSpec(blk, lambda b, hp, g: (b, g * cpg + hp)),
                  pl.BlockSpec(blk, lambda b, hp, g: (b, g * cpg + cpk + hp)),
                  pl.BlockSpec(blk, lambda b, hp, g: (b, g * cpg + 2 * cpk + hp))],
        out_specs=pl.BlockSpec(blk, lambda b, hp, g: (b, hp)),
        out_shape=jax.ShapeDtypeStruct((m, ATTN_WIDTH), BF16),
        scratch_shapes=[pltpu.VMEM(blk, F32)] * 3,
        compiler_params=_cparams(("arbitrary", "arbitrary", "arbitrary")),
        name="attn_prompt",
    )(qkv, qkv, qkv)


def _attn_sample_kernel(qkv_ref, c0_ref, c1_ref, c2_ref, e_ref, et_ref, o_ref):
    e = e_ref[...]
    et = et_ref[...]
    outs, lses = [], []
    for gi, c_ref in enumerate((c0_ref, c1_ref, c2_ref)):
        base = gi * QKV_WIDTH
        q = qkv_ref[:, base:base + ATTN_WIDTH]
        kn = qkv_ref[:, base + ATTN_WIDTH:base + 2 * ATTN_WIDTH]
        vn = qkv_ref[:, base + 2 * ATTN_WIDTH:base + 3 * ATTN_WIDTH]
        kc = c_ref[:, 0:ATTN_WIDTH]
        vc = c_ref[:, ATTN_WIDTH:2 * ATTN_WIDTH]
        prod = jnp.concatenate([kc * q, jnp.broadcast_to(kn * q, (8, ATTN_WIDTH))], axis=0)
        s = _dot_exact(prod, e)
        mx = jnp.max(s, axis=0, keepdims=True)
        p = jnp.exp(s - mx)
        den = jnp.sum(p[0:NSTEP], axis=0, keepdims=True) + p[NSTEP:NSTEP + 1]
        pe = _dot(p.astype(BF16), et)
        o = jnp.sum(pe[0:NSTEP] * vc, axis=0, keepdims=True) + pe[NSTEP:NSTEP + 1] * vn
        inv = _dot_exact(jnp.broadcast_to(1.0 / den, (8, LANES)), et)[0:1]
        outs.append(o * inv)
        lses.append(mx + jnp.log(den))
    top = jnp.maximum(jnp.maximum(lses[0], lses[1]), lses[2])
    ws = [jnp.exp(l - top) for l in lses]
    tot = ws[0] + ws[1] + ws[2]
    acc = jnp.zeros((1, ATTN_WIDTH), F32)
    for w, o in zip(ws, outs):
        acc = acc + _dot_exact(jnp.broadcast_to(w / tot, (8, LANES)), et)[0:1] * o
    o_ref[...] = acc


def _attention_sample(qkv, caches, layer, e, et):
    bd = qkv.shape[0]
    cache_specs, cache_args = [], []
    for (win, dil), c in zip(ATTN_GROUPS, caches):
        width = 2 * ATTN_WIDTH
        cache_args.append(c.reshape(c.shape[0], bd, c.shape[2] // dil, dil * width))
        cache_specs.append(pl.BlockSpec((None, None, NSTEP, width),
                                        lambda b, layer=layer: (layer, b, 0, 0)))
    out = pl.pallas_call(
        _attn_sample_kernel,
        grid=(bd,),
        in_specs=[pl.BlockSpec((None, 1, N_GROUPS * QKV_WIDTH), lambda b: (b, 0, 0))] + cache_specs
                 + [pl.BlockSpec((ATTN_WIDTH, LANES), lambda b: (0, 0)),
                    pl.BlockSpec((LANES, ATTN_WIDTH), lambda b: (0, 0))],
        out_specs=pl.BlockSpec((None, 1, ATTN_WIDTH), lambda b: (b, 0, 0)),
        out_shape=jax.ShapeDtypeStruct((bd, 1, ATTN_WIDTH), F32),
        compiler_params=_cparams(("arbitrary",)),
        name="attn_sample",
    )(qkv.reshape(bd, 1, -1), *cache_args, e, et)
    return out.reshape(bd, ATTN_WIDTH)


def _out_ln_kernel(a_ref, w_ref, x_ref, g_ref, b_ref, o_ref):
    y = _dot(a_ref[...].astype(BF16), w_ref[...]) + ALPHA * x_ref[...]
    o_ref[...] = _layer_norm(y, g_ref[...], b_ref[...])


def _out_ln(a, w, x, g, b, tm):
    m, k = a.shape
    d = w.shape[1]
    return pl.pallas_call(
        _out_ln_kernel,
        grid=(m // tm,),
        in_specs=[pl.BlockSpec((tm, k), lambda i: (i, 0)),
                  pl.BlockSpec((k, d), lambda i: (0, 0)),
                  pl.BlockSpec((tm, d), lambda i: (i, 0)),
                  pl.BlockSpec((1, d), lambda i: (0, 0)),
                  pl.BlockSpec((1, d), lambda i: (0, 0))],
        out_specs=pl.BlockSpec((tm, d), lambda i: (i, 0)),
        out_shape=jax.ShapeDtypeStruct((m, d), F32),
        compiler_params=_cparams(("arbitrary",)),
        name="out_ln",
    )(a, w, x, g.reshape(1, d), b.reshape(1, d))


def _mlp_kernel(x_ref, w1_ref, w2_ref, g_ref, b_ref, o_ref, xb_s, acc_s):
    f = pl.program_id(1)

    @pl.when(f == 0)
    def _():
        xb_s[...] = x_ref[...].astype(BF16)

    h = jnp.maximum(_dot(xb_s[...], w1_ref[...]), 0.0)
    part = _dot((h * h).astype(BF16), w2_ref[...])

    @pl.when(f == 0)
    def _():
        acc_s[...] = part

    @pl.when(f > 0)
    def _():
        acc_s[...] += part

    @pl.when(f == pl.num_programs(1) - 1)
    def _():
        o_ref[...] = _layer_norm(acc_s[...] + ALPHA * x_ref[...], g_ref[...], b_ref[...])


def _mlp(x, w1, w2, g, b, tm, tf=1024):
    m, d = x.shape
    ff = w1.shape[1]
    return pl.pallas_call(
        _mlp_kernel,
        grid=(m // tm, ff // tf),
        in_specs=[pl.BlockSpec((tm, d), lambda i, f: (i, 0)),
                  pl.BlockSpec((d, tf), lambda i, f: (0, f)),
                  pl.BlockSpec((tf, d), lambda i, f: (f, 0)),
                  pl.BlockSpec((1, d), lambda i, f: (0, 0)),
                  pl.BlockSpec((1, d), lambda i, f: (0, 0))],
        out_specs=pl.BlockSpec((tm, d), lambda i, f: (i, 0)),
        out_shape=jax.ShapeDtypeStruct((m, d), F32),
        scratch_shapes=[pltpu.VMEM((tm, d), BF16), pltpu.VMEM((tm, d), F32)],
        compiler_params=_cparams(("arbitrary", "arbitrary")),
        name="mlp",
    )(x, w1, w2, g.reshape(1, d), b.reshape(1, d))


def _gate_norm(y, z, nw):
    y = y * _silu(z)
    ms = jnp.mean(y * y, axis=-1, keepdims=True)
    return y * lax.rsqrt(ms + RMS_EPS) * nw


def _ssd_kernel(x_ref, b_ref, c_ref, z_ref, dt_ref,
                cwx_ref, cwb_ref, cwc_ref, cbx_ref, cbb_ref, cbc_ref,
                dtb_ref, alog_ref, dsk_ref, nw_ref,
                y_ref, st_ref, cvx_ref, cvb_ref, cvc_ref,
                ex_s, eb_s, ec_s, h_s):
    g = pl.program_id(1)
    c = pl.program_id(2)
    L = SSD_CHUNK

    @pl.when(c == 0)
    def _():
        ex_s[0:8, :] = jnp.zeros((8, GROUP_WIDTH), F32)
        eb_s[0:8, :] = jnp.zeros((8, D_STATE), F32)
        ec_s[0:8, :] = jnp.zeros((8, D_STATE), F32)
        h_s[...] = jnp.zeros_like(h_s)

    def conv(raw_ref, e_s, cw_ref, cb_ref):
        e_s[8:8 + L, :] = raw_ref[...]
        acc = cb_ref[...] + cw_ref[0:1, :] * e_s[5:5 + L, :]
        for k in range(1, CONV_K):
            acc = acc + cw_ref[k:k + 1, :] * e_s[5 + k:5 + k + L, :]
        return _silu(acc)

    xs = conv(x_ref, ex_s, cwx_ref, cbx_ref)
    bm = conv(b_ref, eb_s, cwb_ref, cbb_ref)
    cm = conv(c_ref, ec_s, cwc_ref, cbc_ref)

    @pl.when(c == pl.num_programs(2) - 1)
    def _():
        cvx_ref[...] = ex_s[5 + L:8 + L, :]
        cvb_ref[...] = eb_s[5 + L:8 + L, :]
        cvc_ref[...] = ec_s[5 + L:8 + L, :]

    ex_s[0:8, :] = ex_s[L:L + 8, :]
    eb_s[0:8, :] = eb_s[L:L + 8, :]
    ec_s[0:8, :] = ec_s[L:L + 8, :]

    shift = (LANES - HEADS_PER_GROUP * g) % LANES
    dt = _softplus(pltpu.roll(dt_ref[...], shift, 1) + dtb_ref[...])
    a = -jnp.exp(alog_ref[...])
    row = lax.broadcasted_iota(jnp.int32, (L, L), 0)
    col = lax.broadcasted_iota(jnp.int32, (L, L), 1)
    causal = row >= col
    tril = jnp.where(causal, 1.0, 0.0).astype(BF16)
    acum = _dot_exact_rhs(tril, dt * a)
    acum_t = acum.T

    bmb = bm.astype(BF16)
    cmb = cm.astype(BF16)
    cb = _dot_nt(cmb, bmb)
    lane_a = col < SSM_HEAD_DIM
    row_a = row < SSM_HEAD_DIM

    for pair in range(HEADS_PER_GROUP // 2):
        ha, hb = 2 * pair, 2 * pair + 1
        xp = xs[:, pair * LANES:(pair + 1) * LANES]
        dt_p = jnp.where(lane_a, dt[:, ha:ha + 1], dt[:, hb:hb + 1])
        xdt = xp * dt_p
        xdtb = xdt.astype(BF16)

        def decay_of(hd):
            diff = acum[:, hd:hd + 1] - acum_t[hd:hd + 1, :]
            return jnp.exp(jnp.where(causal, diff, NEG))

        y_a = _dot((cb * decay_of(ha)).astype(BF16), xdtb)
        y_b = _dot((cb * decay_of(hb)).astype(BF16), xdtb)
        y_diag = jnp.where(lane_a, y_a, y_b)

        h_in = h_s[pair * LANES:(pair + 1) * LANES, :]
        ac_p = jnp.where(lane_a, acum[:, ha:ha + 1], acum[:, hb:hb + 1])
        y_off = _dot_nt(cmb, h_in.astype(BF16)) * jnp.exp(ac_p)

        last_p = jnp.where(lane_a, acum[L - 1:L, ha:ha + 1], acum[L - 1:L, hb:hb + 1])
        xw = xdt * jnp.exp(last_p - ac_p)
        upd = _dot(xw.T.astype(BF16), bmb)
        last_r = jnp.where(row_a, acum[L - 1:L, ha:ha + 1], acum[L - 1:L, hb:hb + 1])
        h_s[pair * LANES:(pair + 1) * LANES, :] = jnp.exp(last_r) * h_in + upd

        y = y_diag + y_off + xp * dsk_ref[:, pair * LANES:(pair + 1) * LANES]
        y_ref[:, pair * LANES:(pair + 1) * LANES] = y

    y_ref[...] = _gate_norm(y_ref[...], z_ref[...], nw_ref[...])

    @pl.when(c == pl.num_programs(2) - 1)
    def _():
        st_ref[...] = h_s[...]


def _dot_exact_rhs(e, x):
    hi = x.astype(BF16)
    r1 = x - hi.astype(F32)
    mid = r1.astype(BF16)
    lo = (r1 - mid.astype(F32)).astype(BF16)
    return _dot(e, hi) + _dot(e, mid) + _dot(e, lo)


def _ssd_prompt(proj, prm, batch, seq):
    m = proj.shape[0]
    nc = seq // SSD_CHUNK
    L = SSD_CHUNK
    gw, ds = GROUP_WIDTH, D_STATE
    xcol = D_INNER // gw
    bcol = (2 * D_INNER) // ds
    ccol = bcol + SSM_GROUPS
    dtcol = DT_COL // LANES
    row = lambda b, g, c: b * nc + c
    in_specs = [
        pl.BlockSpec((L, gw), lambda b, g, c: (row(b, g, c), xcol + g)),
        pl.BlockSpec((L, ds), lambda b, g, c: (row(b, g, c), bcol + g)),
        pl.BlockSpec((L, ds), lambda b, g, c: (row(b, g, c), ccol + g)),
        pl.BlockSpec((L, gw), lambda b, g, c: (row(b, g, c), g)),
        pl.BlockSpec((L, LANES), lambda b, g, c: (row(b, g, c), dtcol)),
        pl.BlockSpec((CONV_K, gw), lambda b, g, c: (0, g)),
        pl.BlockSpec((CONV_K, ds), lambda b, g, c: (0, D_INNER // ds + g)),
        pl.BlockSpec((CONV_K, ds), lambda b, g, c: (0, D_INNER // ds + SSM_GROUPS + g)),
        pl.BlockSpec((1, gw), lambda b, g, c: (0, g)),
        pl.BlockSpec((1, ds), lambda b, g, c: (0, D_INNER // ds + g)),
        pl.BlockSpec((1, ds), lambda b, g, c: (0, D_INNER // ds + SSM_GROUPS + g)),
        pl.BlockSpec((None, 1, LANES), lambda b, g, c: (g, 0, 0)),
        pl.BlockSpec((None, 1, LANES), lambda b, g, c: (g, 0, 0)),
        pl.BlockSpec((1, gw), lambda b, g, c: (0, g)),
        pl.BlockSpec((1, gw), lambda b, g, c: (0, g)),
    ]
    out_specs = [
        pl.BlockSpec((L, gw), lambda b, g, c: (row(b, g, c), g)),
        pl.BlockSpec((None, gw, ds), lambda b, g, c: (b, g, 0)),
        pl.BlockSpec((None, CONV_K - 1, gw), lambda b, g, c: (b, 0, g)),
        pl.BlockSpec((None, CONV_K - 1, ds), lambda b, g, c: (b, 0, g)),
        pl.BlockSpec((None, CONV_K - 1, ds), lambda b, g, c: (b, 0, g)),
    ]
    out_shape = [
        jax.ShapeDtypeStruct((m, D_INNER), F32),
        jax.ShapeDtypeStruct((batch, SSM_HEADS * SSM_HEAD_DIM, ds), F32),
        jax.ShapeDtypeStruct((batch, CONV_K - 1, D_INNER), F32),
        jax.ShapeDtypeStruct((batch, CONV_K - 1, SSM_GROUPS * ds), F32),
        jax.ShapeDtypeStruct((batch, CONV_K - 1, SSM_GROUPS * ds), F32),
    ]
    y, st, cvx, cvb, cvc = pl.pallas_call(
        _ssd_kernel,
        grid=(batch, SSM_GROUPS, nc),
        in_specs=in_specs,
        out_specs=out_specs,
        out_shape=out_shape,
        scratch_shapes=[pltpu.VMEM((L + 8, gw), F32), pltpu.VMEM((L + 8, ds), F32),
                        pltpu.VMEM((L + 8, ds), F32), pltpu.VMEM((gw, ds), F32)],
        compiler_params=_cparams(("arbitrary", "arbitrary", "arbitrary")),
        name="ssd_prompt",
    )(proj, proj, proj, proj, proj,
      prm["conv_w"], prm["conv_w"], prm["conv_w"], prm["conv_b"], prm["conv_b"], prm["conv_b"],
      prm["dt_bias_g"], prm["a_log_g"], prm["d_exp"], prm["norm_w"])
    conv = jnp.concatenate([cvx, cvb, cvc], axis=-1)
    return y, st.reshape(batch, SSM_HEADS, SSM_HEAD_DIM, ds), conv


def _ssd_sample_kernel(x_ref, b_ref, c_ref, z_ref, dt_ref, cvx_ref, cvb_ref, cvc_ref, h_ref,
                       cwx_ref, cwb_ref, cwc_ref, cbx_ref, cbb_ref, cbc_ref,
                       dtb_ref, alog_ref, dsk_ref, nw_ref, e4_ref,
                       y_ref, ho_ref, nvx_ref, nvb_ref, nvc_ref, y_s):
    g = pl.program_id(0)
    bd = x_ref.shape[0]

    def conv(raw_ref, st_ref, new_ref, cw_ref, cb_ref):
        raw = raw_ref[...]
        acc = cb_ref[...] + cw_ref[CONV_K - 1:CONV_K, :] * raw
        for k in range(CONV_K - 1):
            acc = acc + cw_ref[k:k + 1, :] * st_ref[k]
        for k in range(CONV_K - 2):
            new_ref[k] = st_ref[k + 1]
        new_ref[CONV_K - 2] = raw
        return _silu(acc)

    xs = conv(x_ref, cvx_ref, nvx_ref, cwx_ref, cbx_ref)
    bm = conv(b_ref, cvb_ref, nvb_ref, cwb_ref, cbb_ref)
    cm = conv(c_ref, cvc_ref, nvc_ref, cwc_ref, cbc_ref)

    shift = (LANES - HEADS_PER_GROUP * g) % LANES
    dt = _softplus(pltpu.roll(dt_ref[...], shift, 1) + dtb_ref[...])
    da = jnp.exp(dt * (-jnp.exp(alog_ref[...])))
    e4 = e4_ref[...]
    dt_e = _dot_exact(dt, e4)
    da_e = _dot_exact(da, e4)
    xdt = xs * dt_e
    pad = jnp.zeros((LANES - bd, GROUP_WIDTH), F32)
    xdt_t = jnp.concatenate([xdt, pad], axis=0).T
    da_t = jnp.concatenate([da_e, pad], axis=0).T
    cmb = cm.astype(BF16)

    for b in range(bd):
        h_new = da_t[:, b:b + 1] * h_ref[b] + xdt_t[:, b:b + 1] * bm[b:b + 1, :]
        ho_ref[b] = h_new
        yb = _dot_nt(jnp.broadcast_to(cmb[b:b + 1, :], (8, D_STATE)), h_new.astype(BF16))
        y_s[b:b + 1, :] = yb[0:1]

    y = y_s[...] + xs * dsk_ref[...]
    y_ref[...] = _gate_norm(y, z_ref[...], nw_ref[...])


def _ssd_sample(proj, conv_state, h0, prm, e4):
    bd = proj.shape[0]
    gw, ds = GROUP_WIDTH, D_STATE
    xcol = D_INNER // gw
    bcol = (2 * D_INNER) // ds
    ccol = bcol + SSM_GROUPS
    dtcol = DT_COL // LANES
    cst = jnp.transpose(conv_state, (1, 0, 2))
    h0v = h0.reshape(bd, SSM_HEADS * SSM_HEAD_DIM, ds)
    kw = D_INNER // ds
    in_specs = [
        pl.BlockSpec((bd, gw), lambda g: (0, xcol + g)),
        pl.BlockSpec((bd, ds), lambda g: (0, bcol + g)),
        pl.BlockSpec((bd, ds), lambda g: (0, ccol + g)),
        pl.BlockSpec((bd, gw), lambda g: (0, g)),
        pl.BlockSpec((bd, LANES), lambda g: (0, dtcol)),
        pl.BlockSpec((CONV_K - 1, bd, gw), lambda g: (0, 0, g)),
        pl.BlockSpec((CONV_K - 1, bd, ds), lambda g: (0, 0, kw + g)),
        pl.BlockSpec((CONV_K - 1, bd, ds), lambda g: (0, 0, kw + SSM_GROUPS + g)),
        pl.BlockSpec((bd, gw, ds), lambda g: (0, g, 0)),
        pl.BlockSpec((CONV_K, gw), lambda g: (0, g)),
        pl.BlockSpec((CONV_K, ds), lambda g: (0, kw + g)),
        pl.BlockSpec((CONV_K, ds), lambda g: (0, kw + SSM_GROUPS + g)),
        pl.BlockSpec((1, gw), lambda g: (0, g)),
        pl.BlockSpec((1, ds), lambda g: (0, kw + g)),
        pl.BlockSpec((1, ds), lambda g: (0, kw + SSM_GROUPS + g)),
        pl.BlockSpec((None, 1, LANES), lambda g: (g, 0, 0)),
        pl.BlockSpec((None, 1, LANES), lambda g: (g, 0, 0)),
        pl.BlockSpec((1, gw), lambda g: (0, g)),
        pl.BlockSpec((1, gw), lambda g: (0, g)),
        pl.BlockSpec((LANES, gw), lambda g: (0, 0)),
    ]
    out_specs = [
        pl.BlockSpec((bd, gw), lambda g: (0, g)),
        pl.BlockSpec((bd, gw, ds), lambda g: (0, g, 0)),
        pl.BlockSpec((CONV_K - 1, bd, gw), lambda g: (0, 0, g)),
        pl.BlockSpec((CONV_K - 1, bd, ds), lambda g: (0, 0, g)),
        pl.BlockSpec((CONV_K - 1, bd, ds), lambda g: (0, 0, g)),
    ]
    out_shape = [
        jax.ShapeDtypeStruct((bd, D_INNER), F32),
        jax.ShapeDtypeStruct((bd, SSM_HEADS * SSM_HEAD_DIM, ds), F32),
        jax.ShapeDtypeStruct((CONV_K - 1, bd, D_INNER), F32),
        jax.ShapeDtypeStruct((CONV_K - 1, bd, SSM_GROUPS * ds), F32),
        jax.ShapeDtypeStruct((CONV_K - 1, bd, SSM_GROUPS * ds), F32),
    ]
    y, hn, nvx, nvb, nvc = pl.pallas_call(
        _ssd_sample_kernel,
        grid=(SSM_GROUPS,),
        in_specs=in_specs,
        out_specs=out_specs,
        out_shape=out_shape,
        scratch_shapes=[pltpu.VMEM((bd, gw), F32)],
        compiler_params=_cparams(("arbitrary",)),
        name="ssd_sample",
    )(proj, proj, proj, proj, proj, cst, cst, cst, h0v,
      prm["conv_w"], prm["conv_w"], prm["conv_w"], prm["conv_b"], prm["conv_b"], prm["conv_b"],
      prm["dt_bias_g"], prm["a_log_g"], prm["d_exp"], prm["norm_w"], e4)
    conv = jnp.transpose(jnp.concatenate([nvx, nvb, nvc], axis=-1), (1, 0, 2))
    return y, hn.reshape(bd, SSM_HEADS, SSM_HEAD_DIM, ds), conv


def _group_lanes(v):
    v = v.reshape(SSM_GROUPS, 1, HEADS_PER_GROUP).astype(F32)
    return jnp.pad(v, ((0, 0), (0, 0), (0, LANES - HEADS_PER_GROUP)))


def kernel(x_prompt, x_sample, cache_kv_w128, cache_kv_w512, cache_kv_w2048, state_ssm, state_conv,
           attn_w_in, attn_w_out, ssm_w_in, ssm_conv_w, ssm_conv_b, ssm_dt_bias, ssm_a_log, ssm_d,
           ssm_norm_w, ssm_w_out, mlp_w1, mlp_w2, ln_mix_g, ln_mix_b, ln_ffn_g, ln_ffn_b):
    caches = (cache_kv_w128, cache_kv_w512, cache_kv_w2048)
    bp, seq, d = x_prompt.shape
    bd = x_sample.shape[0]
    mp = bp * seq
    xp = x_prompt.reshape(mp, d)
    xs = x_sample.reshape(bd, d)

    tm_p = 1024 if seq % 1024 == 0 else seq
    rope_p = _rope_tables(jnp.arange(seq, dtype=jnp.int32))
    rope_s = _rope_tables(jnp.full((bd,), PAST_LEN, dtype=jnp.int32))

    head_of_lane = jnp.arange(ATTN_WIDTH) // HEAD_DIM
    e_heads = (head_of_lane[:, None] == jnp.arange(LANES)[None, :]).astype(BF16)
    et_heads = e_heads.T
    e4 = ((jnp.arange(GROUP_WIDTH) // SSM_HEAD_DIM)[None, :] == jnp.arange(LANES)[:, None]).astype(BF16)

    kv_p = [[] for _ in ATTN_GROUPS]
    kv_new = [[] for _ in ATTN_GROUPS]
    ssm_p, conv_p, ssm_s, conv_s = [], [], [], []

    for i in range(DEPTH):
        j = i // 2
        if i % 2 == 0:
            w_in = attn_w_in[j].astype(BF16)
            w_out = attn_w_out[j].astype(BF16)
            qkv_p = _project(xp, w_in, tm_p, ATTN_WIDTH, rope=rope_p, table_blocks=seq // tm_p)
            qkv_s = _project(xs, w_in, bd, ATTN_WIDTH, rope=rope_s)
            hp = _attention_prompt(qkv_p, bp, seq)
            hs = _attention_sample(qkv_s, caches, j, e_heads, et_heads)
            qkv_p3 = qkv_p.reshape(bp, seq, N_GROUPS * QKV_WIDTH)
            for gi, (win, _) in enumerate(ATTN_GROUPS):
                lo = gi * QKV_WIDTH + ATTN_WIDTH
                keep = min(win, seq)
                kv_p[gi].append(qkv_p3[:, seq - keep:, lo:lo + 2 * ATTN_WIDTH]
                                .reshape(bp, keep, 2, ATTN_HEADS, HEAD_DIM))
                kv_new[gi].append(qkv_s[:, lo:lo + 2 * ATTN_WIDTH])
        else:
            w_in = jnp.pad(ssm_w_in[j], ((0, 0), (0, SSM_IN_PAD - ssm_w_in.shape[2]))).astype(BF16)
            w_out = ssm_w_out[j].astype(BF16)
            prm = {
                "conv_w": ssm_conv_w[j],
                "conv_b": ssm_conv_b[j].reshape(1, CONV_DIM),
                "dt_bias_g": _group_lanes(ssm_dt_bias[j]),
                "a_log_g": _group_lanes(ssm_a_log[j]),
                "d_exp": jnp.repeat(ssm_d[j].astype(F32), SSM_HEAD_DIM).reshape(1, D_INNER),
                "norm_w": ssm_norm_w[j].reshape(1, D_INNER),
            }
            proj_p = _project(xp, w_in, tm_p, 1024)
            proj_s = _project(xs, w_in, bd, 1024)
            hp, st_p, cv_p = _ssd_prompt(proj_p, prm, bp, seq)
            hs, st_s, cv_s = _ssd_sample(proj_s, state_conv[j], state_ssm[j], prm, e4)
            ssm_p.append(st_p)
            conv_p.append(cv_p)
            ssm_s.append(st_s)
            conv_s.append(cv_s)
        xp = _out_ln(hp, w_out, xp, ln_mix_g[i], ln_mix_b[i], 512)
        xs = _out_ln(hs, w_out, xs, ln_mix_g[i], ln_mix_b[i], bd)
        w1 = mlp_w1[i].astype(BF16)
        w2 = mlp_w2[i].astype(BF16)
        xp = _mlp(xp, w1, w2, ln_ffn_g[i], ln_ffn_b[i], tm_p)
        xs = _mlp(xs, w1, w2, ln_ffn_g[i], ln_ffn_b[i], bd)

    kv_p_out = [jnp.stack(t) for t in kv_p]
    kv_s_out = []
    for gi, c in enumerate(caches):
        new = jnp.stack(kv_new[gi]).reshape(c.shape[0], bd, 1, 2, ATTN_HEADS, HEAD_DIM)
        kv_s_out.append(jnp.concatenate([c[:, :, 1:], new], axis=2))
    return (xp.reshape(bp, seq, d), xs.reshape(bd, 1, d),
            kv_p_out[0], kv_p_out[1], kv_p_out[2], jnp.stack(ssm_p), jnp.stack(conv_p),
            kv_s_out[0], kv_s_out[1], kv_s_out[2], jnp.stack(ssm_s), jnp.stack(conv_s))
```

```python
import functools

import jax
import jax.numpy as jnp
from jax import lax
from jax.experimental import pallas as pl
from jax.experimental.pallas import tpu as pltpu

F32 = jnp.float32
BF16 = jnp.bfloat16

D_MODEL = 1024
DEPTH = 4
PAST_LEN = 8192
ATTN_GROUPS = ((128, 1), (512, 4), (2048, 16))
ATTN_UNROLL = (4, 4, 2)
N_GROUPS = len(ATTN_GROUPS)
NSTEP = 128
ATTN_HEADS = 16
HEAD_DIM = 64
ATTN_WIDTH = ATTN_HEADS * HEAD_DIM
QKV_WIDTH = 3 * ATTN_WIDTH
ROPE_THETA = 10000.0
SCALE = HEAD_DIM ** -0.5

D_INNER = 2048
SSM_HEAD_DIM = 64
SSM_HEADS = 32
SSM_GROUPS = 8
HEADS_PER_GROUP = 4
GROUP_WIDTH = HEADS_PER_GROUP * SSM_HEAD_DIM
D_STATE = 128
CONV_K = 4
CONV_DIM = D_INNER + 2 * SSM_GROUPS * D_STATE
DT_COL = D_INNER + CONV_DIM
SSM_IN_PAD = DT_COL + 1024
SSD_CHUNK = 128
SSD_SUBCHUNKS = 2

D_FF = 4096
ALPHA = (2 * DEPTH) ** 0.25
LN_EPS = 1e-5
RMS_EPS = 1e-5

LANES = 128
NEG = -1e30
VMEM_LIMIT = 56 * 1024 * 1024


def _cparams(sem):
    return pltpu.CompilerParams(dimension_semantics=sem, vmem_limit_bytes=VMEM_LIMIT)


def _dot(a, b):
    return jnp.dot(a, b, preferred_element_type=F32)


def _dot_nt(a, b):
    return lax.dot_general(a, b, (((1,), (1,)), ((), ())), preferred_element_type=F32)


def _split3(x):
    hi = x.astype(BF16)
    r1 = x - hi.astype(F32)
    mid = r1.astype(BF16)
    lo = (r1 - mid.astype(F32)).astype(BF16)
    return hi, mid, lo


def _dot_exact(x, e):
    hi, mid, lo = _split3(x)
    return _dot(hi, e) + _dot(mid, e) + _dot(lo, e)


def _dot_exact_rhs(e, x):
    hi, mid, lo = _split3(x)
    return _dot(e, hi) + _dot(e, mid) + _dot(e, lo)


def _silu(x):
    return x / (1.0 + jnp.exp(-x))


def _softplus(x):
    return jnp.maximum(x, 0.0) + jnp.log(1.0 + jnp.exp(-jnp.abs(x)))


def _layer_norm(y, g, b):
    mu = jnp.mean(y, axis=-1, keepdims=True)
    yc = y - mu
    var = jnp.mean(yc * yc, axis=-1, keepdims=True)
    return yc * lax.rsqrt(var + LN_EPS) * g + b


def _proj_kernel(x_ref, w_ref, o_ref):
    o_ref[...] = _dot(x_ref[...].astype(BF16), w_ref[...])


def _proj_rope_kernel(x_ref, w_ref, cos_ref, sa_ref, sb_ref, o_ref):
    xb = x_ref[...].astype(BF16)
    cos, sa, sb = cos_ref[...], sa_ref[...], sb_ref[...]
    chunk = 2 * LANES
    for c in range(ATTN_WIDTH // chunk):
        acc = _dot(xb, w_ref[:, c * chunk:(c + 1) * chunk])
        for h in range(chunk // LANES):
            seg = acc[:, h * LANES:(h + 1) * LANES]
            lo = c * chunk + h * LANES
            o_ref[:, lo:lo + LANES] = (
                seg * cos + pltpu.roll(seg, 96, 1) * sa + pltpu.roll(seg, 32, 1) * sb)


def _project(x, w, tm, tn, rope=None, table_blocks=1):
    m, k = x.shape
    n = w.shape[1]
    grid = (n // tn, m // tm)
    in_specs = [pl.BlockSpec((tm, k), lambda j, i: (i, 0)),
                pl.BlockSpec((k, tn), lambda j, i: (0, j))]
    args = [x, w]
    if rope is None:
        body = _proj_kernel
    else:
        body = _proj_rope_kernel
        tspec = pl.BlockSpec((None, tm, LANES), lambda j, i: (j % 3, i % table_blocks, 0))
        in_specs += [tspec, tspec, tspec]
        args += list(rope)
    return pl.pallas_call(
        body,
        grid=grid,
        in_specs=in_specs,
        out_specs=pl.BlockSpec((tm, tn), lambda j, i: (i, j)),
        out_shape=jax.ShapeDtypeStruct((m, n), F32),
        compiler_params=_cparams(("arbitrary", "arbitrary")),
        name="proj_rope" if rope is not None else "proj",
    )(*args)


def _rope_tables(pos):
    half = HEAD_DIM // 2
    inv = jnp.power(ROPE_THETA, -jnp.arange(half, dtype=F32) * (2.0 / HEAD_DIM))
    ang = pos.astype(F32)[:, None] * inv[None, :]
    cos = jnp.tile(jnp.cos(ang), (1, LANES // half))
    sin = jnp.tile(jnp.sin(ang), (1, LANES // half))
    first = (jnp.arange(LANES) % HEAD_DIM) < half
    sa = jnp.where(first[None, :], -sin, 0.0)
    sb = jnp.where(first[None, :], 0.0, sin)
    one, zero = jnp.ones_like(cos), jnp.zeros_like(cos)
    return (jnp.stack([cos * SCALE, cos, one]), jnp.stack([sa * SCALE, sa, zero]),
            jnp.stack([sb * SCALE, sb, zero]))


def _attn_kernel(q_ref, k_ref, v_ref, o_ref, kv0_ref, kv1_ref, kv2_ref, m_s, l_s, acc_s, *, seq):
    g = pl.program_id(2)
    lane_a = lax.broadcasted_iota(jnp.int32, (NSTEP, LANES), 1) < HEAD_DIM
    kj = lax.broadcasted_iota(jnp.int32, (NSTEP, 2 * NSTEP), 1)
    dmat = kj - lax.broadcasted_iota(jnp.int32, (NSTEP, 2 * NSTEP), 0)
    bias_full = jnp.where(dmat >= 0, jnp.where(dmat <= NSTEP, 0.0, NEG), NEG).astype(F32)
    bias_own = jnp.where(kj >= NSTEP, bias_full, NEG)
    lane_a2 = lax.broadcasted_iota(jnp.int32, (2 * NSTEP, LANES), 1) < HEAD_DIM
    ones = jnp.ones((2 * NSTEP, LANES), BF16)

    def run_group(dil, first, unroll):
        nb = seq // dil // NSTEP
        niter = nb // unroll
        span = unroll * NSTEP

        def rows(start, size):
            if dil == 1:
                return pl.ds(pl.multiple_of(start, NSTEP), size)
            return pl.ds(start, size, stride=dil)

        def step(t, carry):
            r = t // niter
            n0 = (t % niter) * unroll
            base = r + dil * NSTEP * n0
            prev = r + dil * NSTEP * jnp.maximum(n0 - 1, 0)
            q_all = q_ref[rows(base, span), :].astype(BF16)
            k_all = k_ref[rows(base, span), :].astype(BF16)
            v_all = v_ref[rows(base, span), :].astype(BF16)
            k_prev = k_ref[rows(prev, NSTEP), :].astype(BF16)
            v_prev = v_ref[rows(prev, NSTEP), :].astype(BF16)
            bias0 = jnp.where(n0 > 0, bias_full, bias_own)
            zero = jnp.zeros((NSTEP, LANES), BF16)
            chains = []
            for u in range(unroll):
                qb = q_all[u * NSTEP:(u + 1) * NSTEP]
                if u == 0:
                    kw = jnp.concatenate([k_prev, k_all[0:NSTEP]], axis=0)
                    vw = jnp.concatenate([v_prev, v_all[0:NSTEP]], axis=0)
                    bias = bias0
                else:
                    kw = k_all[(u - 1) * NSTEP:(u + 1) * NSTEP]
                    vw = v_all[(u - 1) * NSTEP:(u + 1) * NSTEP]
                    bias = bias_full
                chains.append((jnp.where(lane_a, qb, zero), kw, jnp.where(lane_a2, vw, ones), bias))
                chains.append((jnp.where(lane_a, zero, qb), kw, jnp.where(lane_a2, ones, vw), bias))
            scores = [_dot_nt(qh, kw) + bias for qh, kw, _, bias in chains]
            maxes = [jnp.max(s, axis=1, keepdims=True) for s in scores]
            probs = [jnp.exp(s - mx).astype(BF16) for s, mx in zip(scores, maxes)]
            pvs = [_dot(p, ch[2]) for p, ch in zip(probs, chains)]
            for u in range(unroll):
                o_a, o_b = pvs[2 * u], pvs[2 * u + 1]
                m_blk = jnp.where(lane_a, maxes[2 * u], maxes[2 * u + 1])
                o_blk = jnp.where(lane_a, o_a, o_b)
                l_blk = pltpu.roll(jnp.where(lane_a, o_b, o_a), HEAD_DIM, 1)
                q_rows = rows(base + dil * NSTEP * u, NSTEP)
                if first:
                    m_s[q_rows, :] = m_blk
                    l_s[q_rows, :] = l_blk
                    acc_s[q_rows, :] = o_blk
                else:
                    m_old = m_s[q_rows, :]
                    m_new = jnp.maximum(m_old, m_blk)
                    a_old = jnp.exp(m_old - m_new)
                    a_blk = jnp.exp(m_blk - m_new)
                    m_s[q_rows, :] = m_new
                    l_s[q_rows, :] = a_old * l_s[q_rows, :] + a_blk * l_blk
                    acc_s[q_rows, :] = a_old * acc_s[q_rows, :] + a_blk * o_blk
            return carry

        lax.fori_loop(0, dil * niter, step, 0)

    def write_tail(kv_ref, keep):
        for j in range(keep // LANES):
            r0 = seq - keep + j * LANES
            kt = k_ref[r0:r0 + LANES, :].T
            vt = v_ref[r0:r0 + LANES, :].T
            cols = slice(j * LANES, (j + 1) * LANES)
            kv_ref[0, 0, :, cols] = kt[0:HEAD_DIM]
            kv_ref[0, 1, :, cols] = kt[HEAD_DIM:2 * HEAD_DIM]
            kv_ref[1, 0, :, cols] = vt[0:HEAD_DIM]
            kv_ref[1, 1, :, cols] = vt[HEAD_DIM:2 * HEAD_DIM]

    for gi, ((win, dil), kv_ref) in enumerate(zip(ATTN_GROUPS, (kv0_ref, kv1_ref, kv2_ref))):
        @pl.when(g == gi)
        def _(dil=dil, gi=gi, win=win, kv_ref=kv_ref):
            run_group(dil, gi == 0, ATTN_UNROLL[gi])
            write_tail(kv_ref, min(win, seq))

    @pl.when(g == N_GROUPS - 1)
    def _():
        o_ref[...] = (acc_s[...] / l_s[...]).astype(o_ref.dtype)


def _attention_prompt(qkv, batch, seq):
    m = qkv.shape[0]
    cpg = QKV_WIDTH // LANES
    cpk = ATTN_WIDTH // LANES
    blk = (seq, LANES)
    keeps = [min(win, seq) for win, _ in ATTN_GROUPS]
    return pl.pallas_call(
        functools.partial(_attn_kernel, seq=seq),
        grid=(batch, cpk, N_GROUPS),
        in_specs=[pl.BlockSpec(blk, lambda b, hp, g: (b, g * cpg + hp)),
                  pl.BlockSpec(blk, lambda b, hp, g: (b, g * cpg + cpk + hp)),
                  pl.BlockSpec(blk, lambda b, hp, g: (b, g * cpg + 2 * cpk + hp))],
        out_specs=[pl.BlockSpec(blk, lambda b, hp, g: (b, hp))]
                  + [pl.BlockSpec((None, 2, 2, HEAD_DIM, keep), lambda b, hp, g: (b, 0, hp, 0, 0))
                     for keep in keeps],
        out_shape=[jax.ShapeDtypeStruct((m, ATTN_WIDTH), BF16)]
                  + [jax.ShapeDtypeStruct((batch, 2, ATTN_HEADS, HEAD_DIM, keep), F32) for keep in keeps],
        scratch_shapes=[pltpu.VMEM(blk, F32)] * 3,
        compiler_params=_cparams(("arbitrary", "arbitrary", "arbitrary")),
        name="attn_prompt",
    )(qkv, qkv, qkv)


def _eye64():
    return (lax.broadcasted_iota(jnp.int32, (HEAD_DIM, HEAD_DIM), 0)
            == lax.broadcasted_iota(jnp.int32, (HEAD_DIM, HEAD_DIM), 1))


def _row_to_col(row, eye):
    return jnp.sum(jnp.where(eye, jnp.broadcast_to(row, (HEAD_DIM, HEAD_DIM)), 0.0), axis=1, keepdims=True)


def _col_to_row(col, eye):
    return jnp.sum(jnp.where(eye, jnp.broadcast_to(col, (HEAD_DIM, HEAD_DIM)), 0.0), axis=0, keepdims=True)


def _sample_group(qkv_ref, gi, kt, vt, dil, eye):
    q = _row_to_col(qkv_ref[3 * gi], eye)
    kn = _row_to_col(qkv_ref[3 * gi + 1], eye)
    vn = _row_to_col(qkv_ref[3 * gi + 2], eye)
    s = jnp.sum(q * kt, axis=0, keepdims=True)
    if dil > 1:
        pos = lax.broadcasted_iota(jnp.int32, s.shape, 1)
        s = jnp.where((pos & (dil - 1)) == 0, s, NEG)
    sn = jnp.sum(q * kn, axis=0, keepdims=True)
    mx = jnp.maximum(jnp.max(s, axis=1, keepdims=True), sn)
    p = jnp.exp(s - mx)
    pn = jnp.exp(sn - mx)
    den = jnp.sum(p, axis=1, keepdims=True) + pn
    o = (jnp.sum(p * vt, axis=1, keepdims=True) + pn * vn) / den
    return o, mx + jnp.log(den), kn, vn


def _merge_groups(outs, lses, eye):
    top = jnp.maximum(jnp.maximum(lses[0], lses[1]), lses[2])
    ws = [jnp.exp(l - top) for l in lses]
    acc = ws[0] * outs[0] + ws[1] * outs[1] + ws[2] * outs[2]
    return _col_to_row(acc / (ws[0] + ws[1] + ws[2]), eye)


def _shift_left(x):
    return pltpu.roll(x, x.shape[1] - 1, 1)


def _shift_in(x, new_col):
    lane = lax.broadcasted_iota(jnp.int32, x.shape, 1)
    return jnp.where(lane == x.shape[1] - 1, new_col, _shift_left(x))


def _cache_attn_first_kernel(qkv_ref, c0_ref, c1_ref, c2_ref, o_ref, n0_ref, n1_ref, n2_ref):
    eye = _eye64()
    outs, lses = [], []
    for gi, (c_ref, n_ref) in enumerate(zip((c0_ref, c1_ref, c2_ref), (n0_ref, n1_ref, n2_ref))):
        kt, vt = c_ref[0, 0], c_ref[0, 1]
        o, lse, kn, vn = _sample_group(qkv_ref, gi, kt, vt, ATTN_GROUPS[gi][1], eye)
        outs.append(o)
        lses.append(lse)
        n_ref[0, 0] = _shift_in(kt, kn)
        n_ref[0, 1] = _shift_in(vt, vn)
        for layer in range(1, c_ref.shape[0]):
            n_ref[layer, 0] = _shift_left(c_ref[layer, 0])
            n_ref[layer, 1] = _shift_left(c_ref[layer, 1])
    o_ref[...] = _merge_groups(outs, lses, eye)


def _cache_attn_later_kernel(qkv_ref, c0_ref, c1_ref, c2_ref, a0_ref, a1_ref, a2_ref,
                             o_ref, n0_ref, n1_ref, n2_ref):
    del a0_ref, a1_ref, a2_ref
    eye = _eye64()
    outs, lses = [], []
    for gi, (c_ref, n_ref) in enumerate(zip((c0_ref, c1_ref, c2_ref), (n0_ref, n1_ref, n2_ref))):
        kt, vt = c_ref[0], c_ref[1]
        o, lse, kn, vn = _sample_group(qkv_ref, gi, kt, vt, ATTN_GROUPS[gi][1], eye)
        outs.append(o)
        lses.append(lse)
        win = kt.shape[1]
        n_ref[0] = _shift_in(kt[:, win - LANES:], kn)
        n_ref[1] = _shift_in(vt[:, win - LANES:], vn)
    o_ref[...] = _merge_groups(outs, lses, eye)


def _cache_attention(qkv, caches_t, updated, layer):
    bd = qkv.shape[0]
    q5 = qkv.reshape(bd, 3 * N_GROUPS, ATTN_HEADS, 1, HEAD_DIM)
    q_spec = pl.BlockSpec((None, 3 * N_GROUPS, None, 1, HEAD_DIM), lambda b, h: (b, 0, h, 0, 0))
    o_spec = pl.BlockSpec((None, None, 1, HEAD_DIM), lambda b, h: (b, h, 0, 0))
    o_shape = jax.ShapeDtypeStruct((bd, ATTN_HEADS, 1, HEAD_DIM), F32)
    n_shapes = [jax.ShapeDtypeStruct(c.shape, F32) for c in caches_t]
    if layer == 0:
        c_specs = [pl.BlockSpec((c.shape[0], None, 2, None, HEAD_DIM, c.shape[5]),
                                lambda b, h: (0, b, 0, h, 0, 0)) for c in caches_t]
        res = pl.pallas_call(
            _cache_attn_first_kernel,
            grid=(bd, ATTN_HEADS),
            in_specs=[q_spec] + c_specs,
            out_specs=[o_spec] + c_specs,
            out_shape=[o_shape] + n_shapes,
            compiler_params=_cparams(("arbitrary", "arbitrary")),
            name="cache_attn_first",
        )(q5, *caches_t)
    else:
        c_specs = [pl.BlockSpec((None, None, 2, None, HEAD_DIM, c.shape[5]),
                                lambda b, h, layer=layer: (layer, b, 0, h, 0, 0)) for c in caches_t]
        n_specs = [pl.BlockSpec((None, None, 2, None, HEAD_DIM, LANES),
                                lambda b, h, layer=layer, last=c.shape[5] // LANES - 1: (layer, b, 0, h, 0, last))
                   for c in caches_t]
        any_spec = pl.BlockSpec(memory_space=pl.ANY)
        res = pl.pallas_call(
            _cache_attn_later_kernel,
            grid=(bd, ATTN_HEADS),
            in_specs=[q_spec] + c_specs + [any_spec] * N_GROUPS,
            out_specs=[o_spec] + n_specs,
            out_shape=[o_shape] + n_shapes,
            input_output_aliases={1 + N_GROUPS + gi: 1 + gi for gi in range(N_GROUPS)},
            compiler_params=_cparams(("arbitrary", "arbitrary")),
            name="cache_attn_later",
        )(q5, *caches_t, *updated)
    return res[0].reshape(bd, ATTN_WIDTH), list(res[1:])


def _out_ln_kernel(a_ref, w_ref, x_ref, g_ref, b_ref, o_ref):
    y = _dot(a_ref[...].astype(BF16), w_ref[...]) + ALPHA * x_ref[...]
    o_ref[...] = _layer_norm(y, g_ref[...], b_ref[...])


def _out_ln(a, w, x, g, b, tm):
    m, k = a.shape
    d = w.shape[1]
    return pl.pallas_call(
        _out_ln_kernel,
        grid=(m // tm,),
        in_specs=[pl.BlockSpec((tm, k), lambda i: (i, 0)),
                  pl.BlockSpec((k, d), lambda i: (0, 0)),
                  pl.BlockSpec((tm, d), lambda i: (i, 0)),
                  pl.BlockSpec((1, d), lambda i: (0, 0)),
                  pl.BlockSpec((1, d), lambda i: (0, 0))],
        out_specs=pl.BlockSpec((tm, d), lambda i: (i, 0)),
        out_shape=jax.ShapeDtypeStruct((m, d), F32),
        compiler_params=_cparams(("arbitrary",)),
        name="out_ln",
    )(a, w, x, g.reshape(1, d), b.reshape(1, d))


def _mlp_kernel(x_ref, w1_ref, w2_ref, g_ref, b_ref, o_ref, xb_s, acc_s):
    f = pl.program_id(1)

    @pl.when(f == 0)
    def _():
        xb_s[...] = x_ref[...].astype(BF16)

    h = jnp.maximum(_dot(xb_s[...], w1_ref[...]), 0.0)
    part = _dot((h * h).astype(BF16), w2_ref[...])

    @pl.when(f == 0)
    def _():
        acc_s[...] = part

    @pl.when(f > 0)
    def _():
        acc_s[...] += part

    @pl.when(f == pl.num_programs(1) - 1)
    def _():
        o_ref[...] = _layer_norm(acc_s[...] + ALPHA * x_ref[...], g_ref[...], b_ref[...])


def _mlp(x, w1, w2, g, b, tm, tf=1024):
    m, d = x.shape
    ff = w1.shape[1]
    return pl.pallas_call(
        _mlp_kernel,
        grid=(m // tm, ff // tf),
        in_specs=[pl.BlockSpec((tm, d), lambda i, f: (i, 0)),
                  pl.BlockSpec((d, tf), lambda i, f: (0, f)),
                  pl.BlockSpec((tf, d), lambda i, f: (f, 0)),
                  pl.BlockSpec((1, d), lambda i, f: (0, 0)),
                  pl.BlockSpec((1, d), lambda i, f: (0, 0))],
        out_specs=pl.BlockSpec((tm, d), lambda i, f: (i, 0)),
        out_shape=jax.ShapeDtypeStruct((m, d), F32),
        scratch_shapes=[pltpu.VMEM((tm, d), BF16), pltpu.VMEM((tm, d), F32)],
        compiler_params=_cparams(("arbitrary", "arbitrary")),
        name="mlp",
    )(x, w1, w2, g.reshape(1, d), b.reshape(1, d))


def _gate_norm(y, z, nw):
    y = y * _silu(z)
    ms = jnp.mean(y * y, axis=-1, keepdims=True)
    return y * lax.rsqrt(ms + RMS_EPS) * nw


def _ssd_kernel(x_ref, b_ref, c_ref, z_ref, dt_ref,
                cwx_ref, cwb_ref, cwc_ref, cbx_ref, cbb_ref, cbc_ref,
                dtb_ref, alog_ref, dsk_ref, nw_ref,
                y_ref, st_ref, cvx_ref, cvb_ref, cvc_ref,
                ex_s, eb_s, ec_s, h_s):
    g = pl.program_id(1)
    c = pl.program_id(2)
    L = SSD_CHUNK
    R = x_ref.shape[0]

    @pl.when(c == 0)
    def _():
        ex_s[0:8, :] = jnp.zeros((8, GROUP_WIDTH), F32)
        eb_s[0:8, :] = jnp.zeros((8, D_STATE), F32)
        ec_s[0:8, :] = jnp.zeros((8, D_STATE), F32)
        h_s[...] = jnp.zeros_like(h_s)

    def conv(raw_ref, e_s, cw_ref, cb_ref):
        e_s[8:8 + R, :] = raw_ref[...]
        acc = cb_ref[...] + cw_ref[0:1, :] * e_s[5:5 + R, :]
        for k in range(1, CONV_K):
            acc = acc + cw_ref[k:k + 1, :] * e_s[5 + k:5 + k + R, :]
        return _silu(acc)

    xs = conv(x_ref, ex_s, cwx_ref, cbx_ref)
    bm = conv(b_ref, eb_s, cwb_ref, cbb_ref)
    cm = conv(c_ref, ec_s, cwc_ref, cbc_ref)

    @pl.when(c == pl.num_programs(2) - 1)
    def _():
        cvx_ref[...] = ex_s[5 + R:8 + R, :]
        cvb_ref[...] = eb_s[5 + R:8 + R, :]
        cvc_ref[...] = ec_s[5 + R:8 + R, :]

    ex_s[0:8, :] = ex_s[R:R + 8, :]
    eb_s[0:8, :] = eb_s[R:R + 8, :]
    ec_s[0:8, :] = ec_s[R:R + 8, :]

    shift = (LANES - HEADS_PER_GROUP * g) % LANES
    dt_all = _softplus(pltpu.roll(dt_ref[...], shift, 1) + dtb_ref[...])
    dta_all = dt_all * (-jnp.exp(alog_ref[...]))
    row = lax.broadcasted_iota(jnp.int32, (L, L), 0)
    col = lax.broadcasted_iota(jnp.int32, (L, L), 1)
    causal = row >= col
    tril = jnp.where(causal, 1.0, 0.0).astype(BF16)
    lane_a = col < SSM_HEAD_DIM
    row_a = row < SSM_HEAD_DIM
    npair = HEADS_PER_GROUP // 2
    nsub = R // L
    units = [(sc, pair) for sc in range(nsub) for pair in range(npair)]

    acums = [_dot_exact_rhs(tril, dta_all[sc * L:(sc + 1) * L]) for sc in range(nsub)]
    acum_ts = [a.T for a in acums]
    bmbs = [bm[sc * L:(sc + 1) * L].astype(BF16) for sc in range(nsub)]
    cmbs = [cm[sc * L:(sc + 1) * L].astype(BF16) for sc in range(nsub)]
    cbs = [_dot_nt(c, b) for c, b in zip(cmbs, bmbs)]

    def per_head(a, pair):
        return jnp.where(lane_a, a[:, 2 * pair:2 * pair + 1], a[:, 2 * pair + 1:2 * pair + 2])

    xps = {(sc, p): xs[sc * L:(sc + 1) * L, p * LANES:(p + 1) * LANES] for sc, p in units}
    xdts = {(sc, p): xps[sc, p] * per_head(dt_all[sc * L:(sc + 1) * L], p) for sc, p in units}
    ac_ps = {(sc, p): per_head(acums[sc], p) for sc, p in units}
    masked = {}
    for sc, p in units:
        for hd in (2 * p, 2 * p + 1):
            diff = acums[sc][:, hd:hd + 1] - acum_ts[sc][hd:hd + 1, :]
            masked[sc, hd] = (cbs[sc] * jnp.exp(jnp.where(causal, diff, NEG))).astype(BF16)
    y_diags, upds = {}, {}
    for sc, p in units:
        xdtb = xdts[sc, p].astype(BF16)
        y_diags[sc, p] = jnp.where(lane_a, _dot(masked[sc, 2 * p], xdtb), _dot(masked[sc, 2 * p + 1], xdtb))
        last_p = per_head(acums[sc][L - 1:L, :], p)
        xw = xdts[sc, p] * jnp.exp(last_p - ac_ps[sc, p])
        upds[sc, p] = _dot(xw.T.astype(BF16), bmbs[sc])

    for p in range(npair):
        ps = slice(p * LANES, (p + 1) * LANES)
        h = h_s[ps, :]
        for sc in range(nsub):
            rs = slice(sc * L, (sc + 1) * L)
            y_off = _dot_nt(cmbs[sc], h.astype(BF16)) * jnp.exp(ac_ps[sc, p])
            y_ref[rs, ps] = y_diags[sc, p] + y_off + xps[sc, p] * dsk_ref[:, ps]
            last = acums[sc][L - 1:L, :]
            last_r = jnp.where(row_a, last[:, 2 * p:2 * p + 1], last[:, 2 * p + 1:2 * p + 2])
            h = jnp.exp(last_r) * h + upds[sc, p]
        h_s[ps, :] = h
    y_ref[...] = _gate_norm(y_ref[...], z_ref[...], nw_ref[...])

    @pl.when(c == pl.num_programs(2) - 1)
    def _():
        st_ref[...] = h_s[...]


def _ssd_prompt(proj, prm, batch, seq):
    m = proj.shape[0]
    R = SSD_CHUNK * SSD_SUBCHUNKS if seq % (SSD_CHUNK * SSD_SUBCHUNKS) == 0 else SSD_CHUNK
    nc = seq // R
    gw, ds = GROUP_WIDTH, D_STATE
    xcol = D_INNER // gw
    bcol = (2 * D_INNER) // ds
    ccol = bcol + SSM_GROUPS
    dtcol = DT_COL // LANES
    kw = D_INNER // ds
    row = lambda b, g, c: b * nc + c
    in_specs = [
        pl.BlockSpec((R, gw), lambda b, g, c: (row(b, g, c), xcol + g)),
        pl.BlockSpec((R, ds), lambda b, g, c: (row(b, g, c), bcol + g)),
        pl.BlockSpec((R, ds), lambda b, g, c: (row(b, g, c), ccol + g)),
        pl.BlockSpec((R, gw), lambda b, g, c: (row(b, g, c), g)),
        pl.BlockSpec((R, LANES), lambda b, g, c: (row(b, g, c), dtcol)),
        pl.BlockSpec((CONV_K, gw), lambda b, g, c: (0, g)),
        pl.BlockSpec((CONV_K, ds), lambda b, g, c: (0, kw + g)),
        pl.BlockSpec((CONV_K, ds), lambda b, g, c: (0, kw + SSM_GROUPS + g)),
        pl.BlockSpec((1, gw), lambda b, g, c: (0, g)),
        pl.BlockSpec((1, ds), lambda b, g, c: (0, kw + g)),
        pl.BlockSpec((1, ds), lambda b, g, c: (0, kw + SSM_GROUPS + g)),
        pl.BlockSpec((None, 1, LANES), lambda b, g, c: (g, 0, 0)),
        pl.BlockSpec((None, 1, LANES), lambda b, g, c: (g, 0, 0)),
        pl.BlockSpec((1, gw), lambda b, g, c: (0, g)),
        pl.BlockSpec((1, gw), lambda b, g, c: (0, g)),
    ]
    out_specs = [
        pl.BlockSpec((R, gw), lambda b, g, c: (row(b, g, c), g)),
        pl.BlockSpec((None, gw, ds), lambda b, g, c: (b, g, 0)),
        pl.BlockSpec((None, CONV_K - 1, gw), lambda b, g, c: (b, 0, g)),
        pl.BlockSpec((None, CONV_K - 1, ds), lambda b, g, c: (b, 0, g)),
        pl.BlockSpec((None, CONV_K - 1, ds), lambda b, g, c: (b, 0, g)),
    ]
    out_shape = [
        jax.ShapeDtypeStruct((m, D_INNER), F32),
        jax.ShapeDtypeStruct((batch, SSM_HEADS * SSM_HEAD_DIM, ds), F32),
        jax.ShapeDtypeStruct((batch, CONV_K - 1, D_INNER), F32),
        jax.ShapeDtypeStruct((batch, CONV_K - 1, SSM_GROUPS * ds), F32),
        jax.ShapeDtypeStruct((batch, CONV_K - 1, SSM_GROUPS * ds), F32),
    ]
    y, st, cvx, cvb, cvc = pl.pallas_call(
        _ssd_kernel,
        grid=(batch, SSM_GROUPS, nc),
        in_specs=in_specs,
        out_specs=out_specs,
        out_shape=out_shape,
        scratch_shapes=[pltpu.VMEM((R + 8, gw), F32), pltpu.VMEM((R + 8, ds), F32),
                        pltpu.VMEM((R + 8, ds), F32), pltpu.VMEM((gw, ds), F32)],
        compiler_params=_cparams(("arbitrary", "arbitrary", "arbitrary")),
        name="ssd_prompt",
    )(proj, proj, proj, proj, proj,
      prm["conv_w"], prm["conv_w"], prm["conv_w"], prm["conv_b"], prm["conv_b"], prm["conv_b"],
      prm["dt_bias_g"], prm["a_log_g"], prm["d_exp"], prm["norm_w"])
    conv = jnp.concatenate([cvx, cvb, cvc], axis=-1)
    return y, st.reshape(batch, SSM_HEADS, SSM_HEAD_DIM, ds), conv


def _ssd_sample_kernel(x_ref, b_ref, c_ref, z_ref, dt_ref, cvx_ref, cvb_ref, cvc_ref, h_ref,
                       cwx_ref, cwb_ref, cwc_ref, cbx_ref, cbb_ref, cbc_ref,
                       dtb_ref, alog_ref, dsk_ref, nw_ref, e4_ref,
                       y_ref, ho_ref, nvx_ref, nvb_ref, nvc_ref, y_s):
    g = pl.program_id(0)
    bd = x_ref.shape[0]

    def conv(raw_ref, st_ref, new_ref, cw_ref, cb_ref):
        raw = raw_ref[...]
        acc = cb_ref[...] + cw_ref[CONV_K - 1:CONV_K, :] * raw
        for k in range(CONV_K - 1):
            acc = acc + cw_ref[k:k + 1, :] * st_ref[k]
        for k in range(CONV_K - 2):
            new_ref[k] = st_ref[k + 1]
        new_ref[CONV_K - 2] = raw
        return _silu(acc)

    xs = conv(x_ref, cvx_ref, nvx_ref, cwx_ref, cbx_ref)
    bm = conv(b_ref, cvb_ref, nvb_ref, cwb_ref, cbb_ref)
    cm = conv(c_ref, cvc_ref, nvc_ref, cwc_ref, cbc_ref)

    shift = (LANES - HEADS_PER_GROUP * g) % LANES
    dt = _softplus(pltpu.roll(dt_ref[...], shift, 1) + dtb_ref[...])
    da = jnp.exp(dt * (-jnp.exp(alog_ref[...])))
    e4 = e4_ref[...]
    dt_e = _dot_exact(dt, e4)
    da_e = _dot_exact(da, e4)
    xdt = xs * dt_e
    pad = jnp.zeros((LANES - bd, GROUP_WIDTH), F32)
    xdt_t = jnp.concatenate([xdt, pad], axis=0).T
    da_t = jnp.concatenate([da_e, pad], axis=0).T
    cmb = cm.astype(BF16)

    for b in range(bd):
        h_new = da_t[:, b:b + 1] * h_ref[b] + xdt_t[:, b:b + 1] * bm[b:b + 1, :]
        ho_ref[b] = h_new
        yb = _dot_nt(jnp.broadcast_to(cmb[b:b + 1, :], (8, D_STATE)), h_new.astype(BF16))
        y_s[b:b + 1, :] = yb[0:1]

    y = y_s[...] + xs * dsk_ref[...]
    y_ref[...] = _gate_norm(y, z_ref[...], nw_ref[...])


def _ssd_sample(proj, conv_state, h0, prm, e4):
    bd = proj.shape[0]
    gw, ds = GROUP_WIDTH, D_STATE
    xcol = D_INNER // gw
    bcol = (2 * D_INNER) // ds
    ccol = bcol + SSM_GROUPS
    dtcol = DT_COL // LANES
    cst = jnp.transpose(conv_state, (1, 0, 2))
    h0v = h0.reshape(bd, SSM_HEADS * SSM_HEAD_DIM, ds)
    kw = D_INNER // ds
    in_specs = [
        pl.BlockSpec((bd, gw), lambda g: (0, xcol + g)),
        pl.BlockSpec((bd, ds), lambda g: (0, bcol + g)),
        pl.BlockSpec((bd, ds), lambda g: (0, ccol + g)),
        pl.BlockSpec((bd, gw), lambda g: (0, g)),
        pl.BlockSpec((bd, LANES), lambda g: (0, dtcol)),
        pl.BlockSpec((CONV_K - 1, bd, gw), lambda g: (0, 0, g)),
        pl.BlockSpec((CONV_K - 1, bd, ds), lambda g: (0, 0, kw + g)),
        pl.BlockSpec((CONV_K - 1, bd, ds), lambda g: (0, 0, kw + SSM_GROUPS + g)),
        pl.BlockSpec((bd, gw, ds), lambda g: (0, g, 0)),
        pl.BlockSpec((CONV_K, gw), lambda g: (0, g)),
        pl.BlockSpec((CONV_K, ds), lambda g: (0, kw + g)),
        pl.BlockSpec((CONV_K, ds), lambda g: (0, kw + SSM_GROUPS + g)),
        pl.BlockSpec((1, gw), lambda g: (0, g)),
        pl.BlockSpec((1, ds), lambda g: (0, kw + g)),
        pl.BlockSpec((1, ds), lambda g: (0, kw + SSM_GROUPS + g)),
        pl.BlockSpec((None, 1, LANES), lambda g: (g, 0, 0)),
        pl.BlockSpec((None, 1, LANES), lambda g: (g, 0, 0)),
        pl.BlockSpec((1, gw), lambda g: (0, g)),
        pl.BlockSpec((1, gw), lambda g: (0, g)),
        pl.BlockSpec((LANES, gw), lambda g: (0, 0)),
    ]
    out_specs = [
        pl.BlockSpec((bd, gw), lambda g: (0, g)),
        pl.BlockSpec((bd, gw, ds), lambda g: (0, g, 0)),
        pl.BlockSpec((CONV_K - 1, bd, gw), lambda g: (0, 0, g)),
        pl.BlockSpec((CONV_K - 1, bd, ds), lambda g: (0, 0, g)),
        pl.BlockSpec((CONV_K - 1, bd, ds), lambda g: (0, 0, g)),
    ]
    out_shape = [
        jax.ShapeDtypeStruct((bd, D_INNER), F32),
        jax.ShapeDtypeStruct((bd, SSM_HEADS * SSM_HEAD_DIM, ds), F32),
        jax.ShapeDtypeStruct((CONV_K - 1, bd, D_INNER), F32),
        jax.ShapeDtypeStruct((CONV_K - 1, bd, SSM_GROUPS * ds), F32),
        jax.ShapeDtypeStruct((CONV_K - 1, bd, SSM_GROUPS * ds), F32),
    ]
    y, hn, nvx, nvb, nvc = pl.pallas_call(
        _ssd_sample_kernel,
        grid=(SSM_GROUPS,),
        in_specs=in_specs,
        out_specs=out_specs,
        out_shape=out_shape,
        scratch_shapes=[pltpu.VMEM((bd, gw), F32)],
        compiler_params=_cparams(("arbitrary",)),
        name="ssd_sample",
    )(proj, proj, proj, proj, proj, cst, cst, cst, h0v,
      prm["conv_w"], prm["conv_w"], prm["conv_w"], prm["conv_b"], prm["conv_b"], prm["conv_b"],
      prm["dt_bias_g"], prm["a_log_g"], prm["d_exp"], prm["norm_w"], e4)
    conv = jnp.transpose(jnp.concatenate([nvx, nvb, nvc], axis=-1), (1, 0, 2))
    return y, hn.reshape(bd, SSM_HEADS, SSM_HEAD_DIM, ds), conv


def _group_lanes(v):
    v = v.reshape(SSM_GROUPS, 1, HEADS_PER_GROUP).astype(F32)
    return jnp.pad(v, ((0, 0), (0, 0), (0, LANES - HEADS_PER_GROUP)))


def kernel(x_prompt, x_sample, cache_kv_w128, cache_kv_w512, cache_kv_w2048, state_ssm, state_conv,
           attn_w_in, attn_w_out, ssm_w_in, ssm_conv_w, ssm_conv_b, ssm_dt_bias, ssm_a_log, ssm_d,
           ssm_norm_w, ssm_w_out, mlp_w1, mlp_w2, ln_mix_g, ln_mix_b, ln_ffn_g, ln_ffn_b):
    caches = (cache_kv_w128, cache_kv_w512, cache_kv_w2048)
    bp, seq, d = x_prompt.shape
    bd = x_sample.shape[0]
    mp = bp * seq
    xp = x_prompt.reshape(mp, d)
    xs = x_sample.reshape(bd, d)

    tm_p = 1024 if seq % 1024 == 0 else seq
    rope_p = _rope_tables(jnp.arange(seq, dtype=jnp.int32))
    rope_s = _rope_tables(jnp.full((bd,), PAST_LEN, dtype=jnp.int32))
    e4 = ((jnp.arange(GROUP_WIDTH) // SSM_HEAD_DIM)[None, :] == jnp.arange(LANES)[:, None]).astype(BF16)

    caches_t = [jnp.transpose(c, (0, 1, 3, 4, 5, 2)) for c in caches]
    updated = None
    kv_p = [[] for _ in ATTN_GROUPS]
    ssm_p, conv_p, ssm_s, conv_s = [], [], [], []

    for i in range(DEPTH):
        j = i // 2
        if i % 2 == 0:
            w_in = attn_w_in[j].astype(BF16)
            w_out = attn_w_out[j].astype(BF16)
            qkv_p = _project(xp, w_in, tm_p, ATTN_WIDTH, rope=rope_p, table_blocks=seq // tm_p)
            qkv_s = _project(xs, w_in, bd, ATTN_WIDTH, rope=rope_s)
            hp, *tails = _attention_prompt(qkv_p, bp, seq)
            hs, updated = _cache_attention(qkv_s, caches_t, updated, j)
            for gi in range(N_GROUPS):
                kv_p[gi].append(tails[gi])
        else:
            w_in = jnp.pad(ssm_w_in[j], ((0, 0), (0, SSM_IN_PAD - ssm_w_in.shape[2]))).astype(BF16)
            w_out = ssm_w_out[j].astype(BF16)
            prm = {
                "conv_w": ssm_conv_w[j],
                "conv_b": ssm_conv_b[j].reshape(1, CONV_DIM),
                "dt_bias_g": _group_lanes(ssm_dt_bias[j]),
                "a_log_g": _group_lanes(ssm_a_log[j]),
                "d_exp": jnp.repeat(ssm_d[j].astype(F32), SSM_HEAD_DIM).reshape(1, D_INNER),
                "norm_w": ssm_norm_w[j].reshape(1, D_INNER),
            }
            proj_p = _project(xp, w_in, tm_p, 1024)
            proj_s = _project(xs, w_in, bd, 1024)
            hp, st_p, cv_p = _ssd_prompt(proj_p, prm, bp, seq)
            hs, st_s, cv_s = _ssd_sample(proj_s, state_conv[j], state_ssm[j], prm, e4)
            ssm_p.append(st_p)
            conv_p.append(cv_p)
            ssm_s.append(st_s)
            conv_s.append(cv_s)
        xp = _out_ln(hp, w_out, xp, ln_mix_g[i], ln_mix_b[i], 512)
        xs = _out_ln(hs, w_out, xs, ln_mix_g[i], ln_mix_b[i], bd)
        w1 = mlp_w1[i].astype(BF16)
        w2 = mlp_w2[i].astype(BF16)
        xp = _mlp(xp, w1, w2, ln_ffn_g[i], ln_ffn_b[i], tm_p)
        xs = _mlp(xs, w1, w2, ln_ffn_g[i], ln_ffn_b[i], bd)

    back = (0, 1, 5, 2, 3, 4)
    kv_p_out = [jnp.transpose(jnp.stack(t), back) for t in kv_p]
    kv_s_out = [jnp.transpose(u, back) for u in updated]
    return (xp.reshape(bp, seq, d), xs.reshape(bd, 1, d),
            kv_p_out[0], kv_p_out[1], kv_p_out[2], jnp.stack(ssm_p), jnp.stack(conv_p),
            kv_s_out[0], kv_s_out[1], kv_s_out[2], jnp.stack(ssm_s), jnp.stack(conv_s))
```

```python
import functools

import jax
import jax.numpy as jnp
from jax import lax
from jax.experimental import pallas as pl
from jax.experimental.pallas import tpu as pltpu

F32 = jnp.float32
BF16 = jnp.bfloat16

D_MODEL = 1024
DEPTH = 4
PAST_LEN = 8192
ATTN_GROUPS = ((128, 1), (512, 4), (2048, 16))
ATTN_UNROLL = ((8, 1), (8, 1), (2, 2))
N_GROUPS = len(ATTN_GROUPS)
NSTEP = 128
ATTN_HEADS = 16
HEAD_DIM = 64
ATTN_WIDTH = ATTN_HEADS * HEAD_DIM
QKV_WIDTH = 3 * ATTN_WIDTH
CACHE_HEADS = 2
ROPE_THETA = 10000.0
SCALE = HEAD_DIM ** -0.5

D_INNER = 2048
SSM_HEAD_DIM = 64
SSM_HEADS = 32
SSM_GROUPS = 8
HEADS_PER_GROUP = 4
GROUP_WIDTH = HEADS_PER_GROUP * SSM_HEAD_DIM
D_STATE = 128
CONV_K = 4
CONV_DIM = D_INNER + 2 * SSM_GROUPS * D_STATE
DT_COL = D_INNER + CONV_DIM
SSM_IN_PAD = DT_COL + 128
SSD_CHUNK = 128
SSD_SUBCHUNKS = 4

D_FF = 4096
ALPHA = (2 * DEPTH) ** 0.25
LN_EPS = 1e-5
RMS_EPS = 1e-5

LANES = 128
NEG = -1e30
VMEM_LIMIT = 56 * 1024 * 1024


def _cparams(sem):
    return pltpu.CompilerParams(dimension_semantics=sem, vmem_limit_bytes=VMEM_LIMIT)


def _dot(a, b):
    return jnp.dot(a, b, preferred_element_type=F32)


def _dot_nt(a, b):
    return lax.dot_general(a, b, (((1,), (1,)), ((), ())), preferred_element_type=F32)


def _split3(x):
    hi = x.astype(BF16)
    r1 = x - hi.astype(F32)
    mid = r1.astype(BF16)
    lo = (r1 - mid.astype(F32)).astype(BF16)
    return hi, mid, lo


def _dot_exact(x, e):
    hi, mid, lo = _split3(x)
    return _dot(hi, e) + _dot(mid, e) + _dot(lo, e)


def _dot_exact_rhs(e, x):
    hi, mid, lo = _split3(x)
    return _dot(e, hi) + _dot(e, mid) + _dot(e, lo)


def _silu(x):
    return x / (1.0 + jnp.exp(-x))


def _softplus(x):
    return jnp.maximum(x, 0.0) + jnp.log(1.0 + jnp.exp(-jnp.abs(x)))


def _layer_norm(y, g, b):
    mu = jnp.mean(y, axis=-1, keepdims=True)
    yc = y - mu
    var = jnp.mean(yc * yc, axis=-1, keepdims=True)
    return yc * lax.rsqrt(var + LN_EPS) * g + b


def _proj_kernel(x_ref, w_ref, o_ref):
    xb = x_ref[...].astype(BF16)
    chunk = 4 * LANES
    n = o_ref.shape[1]
    for lo in range(0, n, chunk):
        hi = min(lo + chunk, n)
        o_ref[:, lo:hi] = _dot(xb, w_ref[:, lo:hi])


def _proj_rope_kernel(x_ref, w_ref, cos_ref, sa_ref, sb_ref, o_ref):
    xb = x_ref[...].astype(BF16)
    chunk = 2 * LANES
    for kind in range(3):
        cos, sa, sb = cos_ref[kind], sa_ref[kind], sb_ref[kind]
        for c in range(ATTN_WIDTH // chunk):
            base = kind * ATTN_WIDTH + c * chunk
            acc = _dot(xb, w_ref[:, base:base + chunk])
            for h in range(chunk // LANES):
                seg = acc[:, h * LANES:(h + 1) * LANES]
                lo = base + h * LANES
                o_ref[:, lo:lo + LANES] = (
                    seg * cos + pltpu.roll(seg, 96, 1) * sa + pltpu.roll(seg, 32, 1) * sb)


def _project(x, w, tm, tn, rope=None, table_blocks=1):
    m, k = x.shape
    n = w.shape[1]
    grid = (n // tn, m // tm)
    in_specs = [pl.BlockSpec((tm, k), lambda j, i: (i, 0)),
                pl.BlockSpec((k, tn), lambda j, i: (0, j))]
    args = [x, w]
    if rope is None:
        body = _proj_kernel
    else:
        body = _proj_rope_kernel
        tspec = pl.BlockSpec((3, tm, LANES), lambda j, i: (0, i % table_blocks, 0))
        in_specs += [tspec, tspec, tspec]
        args += list(rope)
    return pl.pallas_call(
        body,
        grid=grid,
        in_specs=in_specs,
        out_specs=pl.BlockSpec((tm, tn), lambda j, i: (i, j)),
        out_shape=jax.ShapeDtypeStruct((m, n), F32),
        compiler_params=_cparams(("arbitrary", "arbitrary")),
        name="proj_rope" if rope is not None else "proj",
    )(*args)


def _rope_tables(pos):
    half = HEAD_DIM // 2
    inv = jnp.power(ROPE_THETA, -jnp.arange(half, dtype=F32) * (2.0 / HEAD_DIM))
    ang = pos.astype(F32)[:, None] * inv[None, :]
    cos = jnp.tile(jnp.cos(ang), (1, LANES // half))
    sin = jnp.tile(jnp.sin(ang), (1, LANES // half))
    first = (jnp.arange(LANES) % HEAD_DIM) < half
    sa = jnp.where(first[None, :], -sin, 0.0)
    sb = jnp.where(first[None, :], 0.0, sin)
    one, zero = jnp.ones_like(cos), jnp.zeros_like(cos)
    return (jnp.stack([cos * SCALE, cos, one]), jnp.stack([sa * SCALE, sa, zero]),
            jnp.stack([sb * SCALE, sb, zero]))


def _attn_kernel(q_ref, k_ref, v_ref, o_ref, kv0_ref, kv1_ref, kv2_ref, m_s, l_s, acc_s, *, seq):
    g = pl.program_id(2)
    lane_a = lax.broadcasted_iota(jnp.int32, (NSTEP, LANES), 1) < HEAD_DIM
    kj = lax.broadcasted_iota(jnp.int32, (NSTEP, 2 * NSTEP), 1)
    dmat = kj - lax.broadcasted_iota(jnp.int32, (NSTEP, 2 * NSTEP), 0)
    bias_full = jnp.where(dmat >= 0, jnp.where(dmat <= NSTEP, 0.0, NEG), NEG).astype(F32)
    bias_own = jnp.where(kj >= NSTEP, bias_full, NEG)
    lane_a2 = lax.broadcasted_iota(jnp.int32, (2 * NSTEP, LANES), 1) < HEAD_DIM
    ones = jnp.ones((2 * NSTEP, LANES), BF16)

    def run_group(dil, first, unroll, res_per_iter):
        nb = seq // dil // NSTEP
        niter = nb // unroll
        span = unroll * NSTEP

        def rows(start, size):
            if dil == 1:
                return pl.ds(pl.multiple_of(start, NSTEP), size)
            return pl.ds(start, size, stride=dil)

        def step(t, carry):
            n0 = (t % niter) * unroll
            bias0 = jnp.where(n0 > 0, bias_full, bias_own)
            zero = jnp.zeros((NSTEP, LANES), BF16)
            chains, q_rows_of = [], []
            for rr in range(res_per_iter):
                r = (t // niter) * res_per_iter + rr
                base = r + dil * NSTEP * n0
                prev = r + dil * NSTEP * jnp.maximum(n0 - 1, 0)
                q_all = q_ref[rows(base, span), :].astype(BF16)
                k_all = k_ref[rows(base, span), :].astype(BF16)
                v_all = v_ref[rows(base, span), :].astype(BF16)
                k_prev = k_ref[rows(prev, NSTEP), :].astype(BF16)
                v_prev = v_ref[rows(prev, NSTEP), :].astype(BF16)
                for u in range(unroll):
                    qb = q_all[u * NSTEP:(u + 1) * NSTEP]
                    if u == 0:
                        kw = jnp.concatenate([k_prev, k_all[0:NSTEP]], axis=0)
                        vw = jnp.concatenate([v_prev, v_all[0:NSTEP]], axis=0)
                        bias = bias0
                    else:
                        kw = k_all[(u - 1) * NSTEP:(u + 1) * NSTEP]
                        vw = v_all[(u - 1) * NSTEP:(u + 1) * NSTEP]
                        bias = bias_full
                    chains.append((jnp.where(lane_a, qb, zero), kw, jnp.where(lane_a2, vw, ones), bias))
                    chains.append((jnp.where(lane_a, zero, qb), kw, jnp.where(lane_a2, ones, vw), bias))
                    q_rows_of.append(rows(base + dil * NSTEP * u, NSTEP))
            scores = [_dot_nt(qh, kw) + bias for qh, kw, _, bias in chains]
            maxes = [jnp.max(s, axis=1, keepdims=True) for s in scores]
            probs = [jnp.exp(s - mx).astype(BF16) for s, mx in zip(scores, maxes)]
            pvs = [_dot(p, ch[2]) for p, ch in zip(probs, chains)]
            for u, q_rows in enumerate(q_rows_of):
                o_a, o_b = pvs[2 * u], pvs[2 * u + 1]
                m_blk = jnp.where(lane_a, maxes[2 * u], maxes[2 * u + 1])
                o_blk = jnp.where(lane_a, o_a, o_b)
                l_blk = pltpu.roll(jnp.where(lane_a, o_b, o_a), HEAD_DIM, 1)
                if first:
                    m_s[q_rows, :] = m_blk
                    l_s[q_rows, :] = l_blk
                    acc_s[q_rows, :] = o_blk
                else:
                    m_old = m_s[q_rows, :]
                    m_new = jnp.maximum(m_old, m_blk)
                    a_old = jnp.exp(m_old - m_new)
                    a_blk = jnp.exp(m_blk - m_new)
                    m_s[q_rows, :] = m_new
                    l_s[q_rows, :] = a_old * l_s[q_rows, :] + a_blk * l_blk
                    acc_s[q_rows, :] = a_old * acc_s[q_rows, :] + a_blk * o_blk
            return carry

        lax.fori_loop(0, (dil // res_per_iter) * niter, step, 0)

    def write_tail(kv_ref, keep):
        for j in range(keep // LANES):
            r0 = seq - keep + j * LANES
            kt = k_ref[r0:r0 + LANES, :].T
            vt = v_ref[r0:r0 + LANES, :].T
            cols = slice(j * LANES, (j + 1) * LANES)
            kv_ref[0, 0, :, cols] = kt[0:HEAD_DIM]
            kv_ref[0, 1, :, cols] = kt[HEAD_DIM:2 * HEAD_DIM]
            kv_ref[1, 0, :, cols] = vt[0:HEAD_DIM]
            kv_ref[1, 1, :, cols] = vt[HEAD_DIM:2 * HEAD_DIM]

    for gi, ((win, dil), kv_ref) in enumerate(zip(ATTN_GROUPS, (kv0_ref, kv1_ref, kv2_ref))):
        @pl.when(g == gi)
        def _(dil=dil, gi=gi, win=win, kv_ref=kv_ref):
            run_group(dil, gi == 0, *ATTN_UNROLL[gi])
            write_tail(kv_ref, min(win, seq))

    @pl.when(g == N_GROUPS - 1)
    def _():
        o_ref[...] = (acc_s[...] / l_s[...]).astype(o_ref.dtype)


def _attention_prompt(qkv, batch, seq):
    m = qkv.shape[0]
    cpg = QKV_WIDTH // LANES
    cpk = ATTN_WIDTH // LANES
    blk = (seq, LANES)
    keeps = [min(win, seq) for win, _ in ATTN_GROUPS]
    return pl.pallas_call(
        functools.partial(_attn_kernel, seq=seq),
        grid=(batch, cpk, N_GROUPS),
        in_specs=[pl.BlockSpec(blk, lambda b, hp, g: (b, g * cpg + hp)),
                  pl.BlockSpec(blk, lambda b, hp, g: (b, g * cpg + cpk + hp)),
                  pl.BlockSpec(blk, lambda b, hp, g: (b, g * cpg + 2 * cpk + hp))],
        out_specs=[pl.BlockSpec(blk, lambda b, hp, g: (b, hp))]
                  + [pl.BlockSpec((None, 2, 2, HEAD_DIM, keep), lambda b, hp, g: (b, 0, hp, 0, 0))
                     for keep in keeps],
        out_shape=[jax.ShapeDtypeStruct((m, ATTN_WIDTH), BF16)]
                  + [jax.ShapeDtypeStruct((batch, 2, ATTN_HEADS, HEAD_DIM, keep), F32) for keep in keeps],
        scratch_shapes=[pltpu.VMEM(blk, F32)] * 3,
        compiler_params=_cparams(("arbitrary", "arbitrary", "arbitrary")),
        name="attn_prompt",
    )(qkv, qkv, qkv)


def _eye64():
    return (lax.broadcasted_iota(jnp.int32, (HEAD_DIM, HEAD_DIM), 0)
            == lax.broadcasted_iota(jnp.int32, (HEAD_DIM, HEAD_DIM), 1))


def _row_to_col(row, eye):
    return jnp.sum(jnp.where(eye, jnp.broadcast_to(row, (HEAD_DIM, HEAD_DIM)), 0.0), axis=1, keepdims=True)


def _col_to_row(col, eye):
    return jnp.sum(jnp.where(eye, jnp.broadcast_to(col, (HEAD_DIM, HEAD_DIM)), 0.0), axis=0, keepdims=True)


def _sample_group(qkv_ref, gi, hh, kt, vt, dil, eye):
    q = _row_to_col(qkv_ref[3 * gi, hh], eye)
    kn = _row_to_col(qkv_ref[3 * gi + 1, hh], eye)
    vn = _row_to_col(qkv_ref[3 * gi + 2, hh], eye)
    s = jnp.sum(q * kt, axis=0, keepdims=True)
    if dil > 1:
        pos = lax.broadcasted_iota(jnp.int32, s.shape, 1)
        s = jnp.where((pos & (dil - 1)) == 0, s, NEG)
    sn = jnp.sum(q * kn, axis=0, keepdims=True)
    mx = jnp.maximum(jnp.max(s, axis=1, keepdims=True), sn)
    p = jnp.exp(s - mx)
    pn = jnp.exp(sn - mx)
    den = jnp.sum(p, axis=1, keepdims=True) + pn
    o = (jnp.sum(p * vt, axis=1, keepdims=True) + pn * vn) / den
    return o, mx + jnp.log(den), kn, vn


def _merge_groups(outs, lses, eye):
    top = jnp.maximum(jnp.maximum(lses[0], lses[1]), lses[2])
    ws = [jnp.exp(l - top) for l in lses]
    acc = ws[0] * outs[0] + ws[1] * outs[1] + ws[2] * outs[2]
    return _col_to_row(acc / (ws[0] + ws[1] + ws[2]), eye)


def _shift_left(x):
    return pltpu.roll(x, x.shape[1] - 1, 1)


def _shift_in(x, new_col):
    lane = lax.broadcasted_iota(jnp.int32, x.shape, 1)
    return jnp.where(lane == x.shape[1] - 1, new_col, _shift_left(x))


def _cache_attn_first_kernel(qkv_ref, c0_ref, c1_ref, c2_ref, o_ref, n0_ref, n1_ref, n2_ref):
    eye = _eye64()
    for hh in range(CACHE_HEADS):
        outs, lses = [], []
        for gi, (c_ref, n_ref) in enumerate(zip((c0_ref, c1_ref, c2_ref), (n0_ref, n1_ref, n2_ref))):
            kt, vt = c_ref[0, 0, hh], c_ref[0, 1, hh]
            o, lse, kn, vn = _sample_group(qkv_ref, gi, hh, kt, vt, ATTN_GROUPS[gi][1], eye)
            outs.append(o)
            lses.append(lse)
            n_ref[0, 0, hh] = _shift_in(kt, kn)
            n_ref[0, 1, hh] = _shift_in(vt, vn)
            for layer in range(1, c_ref.shape[0]):
                n_ref[layer, 0, hh] = _shift_left(c_ref[layer, 0, hh])
                n_ref[layer, 1, hh] = _shift_left(c_ref[layer, 1, hh])
        o_ref[hh] = _merge_groups(outs, lses, eye)


def _cache_attn_later_kernel(qkv_ref, c0_ref, c1_ref, c2_ref, a0_ref, a1_ref, a2_ref,
                             o_ref, n0_ref, n1_ref, n2_ref):
    del a0_ref, a1_ref, a2_ref
    eye = _eye64()
    for hh in range(CACHE_HEADS):
        outs, lses = [], []
        for gi, (c_ref, n_ref) in enumerate(zip((c0_ref, c1_ref, c2_ref), (n0_ref, n1_ref, n2_ref))):
            kt, vt = c_ref[0, hh], c_ref[1, hh]
            o, lse, kn, vn = _sample_group(qkv_ref, gi, hh, kt, vt, ATTN_GROUPS[gi][1], eye)
            outs.append(o)
            lses.append(lse)
            win = kt.shape[1]
            n_ref[0, hh] = _shift_in(kt[:, win - LANES:], kn)
            n_ref[1, hh] = _shift_in(vt[:, win - LANES:], vn)
        o_ref[hh] = _merge_groups(outs, lses, eye)


def _cache_attention(qkv, caches_t, updated, layer):
    bd = qkv.shape[0]
    q5 = qkv.reshape(bd, 3 * N_GROUPS, ATTN_HEADS, 1, HEAD_DIM)
    hps = CACHE_HEADS
    grid = (bd, ATTN_HEADS // hps)
    q_spec = pl.BlockSpec((None, 3 * N_GROUPS, hps, 1, HEAD_DIM), lambda b, h: (b, 0, h, 0, 0))
    o_spec = pl.BlockSpec((None, hps, 1, HEAD_DIM), lambda b, h: (b, h, 0, 0))
    o_shape = jax.ShapeDtypeStruct((bd, ATTN_HEADS, 1, HEAD_DIM), F32)
    n_shapes = [jax.ShapeDtypeStruct(c.shape, F32) for c in caches_t]
    if layer == 0:
        c_specs = [pl.BlockSpec((c.shape[0], None, 2, hps, HEAD_DIM, c.shape[5]),
                                lambda b, h: (0, b, 0, h, 0, 0)) for c in caches_t]
        res = pl.pallas_call(
            _cache_attn_first_kernel,
            grid=grid,
            in_specs=[q_spec] + c_specs,
            out_specs=[o_spec] + c_specs,
            out_shape=[o_shape] + n_shapes,
            compiler_params=_cparams(("arbitrary", "arbitrary")),
            name="cache_attn_first",
        )(q5, *caches_t)
    else:
        c_specs = [pl.BlockSpec((None, None, 2, hps, HEAD_DIM, c.shape[5]),
                                lambda b, h, layer=layer: (layer, b, 0, h, 0, 0)) for c in caches_t]
        n_specs = [pl.BlockSpec((None, None, 2, hps, HEAD_DIM, LANES),
                                lambda b, h, layer=layer, last=c.shape[5] // LANES - 1: (layer, b, 0, h, 0, last))
                   for c in caches_t]
        any_spec = pl.BlockSpec(memory_space=pl.ANY)
        res = pl.pallas_call(
            _cache_attn_later_kernel,
            grid=grid,
            in_specs=[q_spec] + c_specs + [any_spec] * N_GROUPS,
            out_specs=[o_spec] + n_specs,
            out_shape=[o_shape] + n_shapes,
            input_output_aliases={1 + N_GROUPS + gi: 1 + gi for gi in range(N_GROUPS)},
            compiler_params=_cparams(("arbitrary", "arbitrary")),
            name="cache_attn_later",
        )(q5, *caches_t, *updated)
    return res[0].reshape(bd, ATTN_WIDTH), list(res[1:])


def _out_ln_kernel(a_ref, w_ref, x_ref, g_ref, b_ref, o_ref):
    y = _dot(a_ref[...].astype(BF16), w_ref[...]) + ALPHA * x_ref[...]
    o_ref[...] = _layer_norm(y, g_ref[...], b_ref[...])


def _out_ln(a, w, x, g, b, tm):
    m, k = a.shape
    d = w.shape[1]
    return pl.pallas_call(
        _out_ln_kernel,
        grid=(m // tm,),
        in_specs=[pl.BlockSpec((tm, k), lambda i: (i, 0)),
                  pl.BlockSpec((k, d), lambda i: (0, 0)),
                  pl.BlockSpec((tm, d), lambda i: (i, 0)),
                  pl.BlockSpec((1, d), lambda i: (0, 0)),
                  pl.BlockSpec((1, d), lambda i: (0, 0))],
        out_specs=pl.BlockSpec((tm, d), lambda i: (i, 0)),
        out_shape=jax.ShapeDtypeStruct((m, d), F32),
        compiler_params=_cparams(("arbitrary",)),
        name="out_ln",
    )(a, w, x, g.reshape(1, d), b.reshape(1, d))


def _mlp_kernel(x_ref, w1_ref, w2_ref, g_ref, b_ref, o_ref, xb_s, acc_s):
    f = pl.program_id(1)

    @pl.when(f == 0)
    def _():
        xb_s[...] = x_ref[...].astype(BF16)

    h = jnp.maximum(_dot(xb_s[...], w1_ref[...]), 0.0)
    part = _dot((h * h).astype(BF16), w2_ref[...])

    @pl.when(f == 0)
    def _():
        acc_s[...] = part

    @pl.when(f > 0)
    def _():
        acc_s[...] += part

    @pl.when(f == pl.num_programs(1) - 1)
    def _():
        o_ref[...] = _layer_norm(acc_s[...] + ALPHA * x_ref[...], g_ref[...], b_ref[...])


def _mlp(x, w1, w2, g, b, tm, tf=1024):
    m, d = x.shape
    ff = w1.shape[1]
    return pl.pallas_call(
        _mlp_kernel,
        grid=(m // tm, ff // tf),
        in_specs=[pl.BlockSpec((tm, d), lambda i, f: (i, 0)),
                  pl.BlockSpec((d, tf), lambda i, f: (0, f)),
                  pl.BlockSpec((tf, d), lambda i, f: (f, 0)),
                  pl.BlockSpec((1, d), lambda i, f: (0, 0)),
                  pl.BlockSpec((1, d), lambda i, f: (0, 0))],
        out_specs=pl.BlockSpec((tm, d), lambda i, f: (i, 0)),
        out_shape=jax.ShapeDtypeStruct((m, d), F32),
        scratch_shapes=[pltpu.VMEM((tm, d), BF16), pltpu.VMEM((tm, d), F32)],
        compiler_params=_cparams(("arbitrary", "arbitrary")),
        name="mlp",
    )(x, w1, w2, g.reshape(1, d), b.reshape(1, d))


def _gate_norm(y, z, nw):
    y = y * _silu(z)
    ms = jnp.mean(y * y, axis=-1, keepdims=True)
    return y * lax.rsqrt(ms + RMS_EPS) * nw


def _ssd_kernel(x_ref, b_ref, c_ref, z_ref, dt_ref,
                cwx_ref, cwb_ref, cwc_ref, cbx_ref, cbb_ref, cbc_ref,
                dtb_ref, alog_ref, dsk_ref, nw_ref,
                y_ref, st_ref, cvx_ref, cvb_ref, cvc_ref,
                ex_s, eb_s, ec_s, h_s):
    g = pl.program_id(1)
    c = pl.program_id(2)
    L = SSD_CHUNK
    R = x_ref.shape[0]

    @pl.when(c == 0)
    def _():
        ex_s[0:8, :] = jnp.zeros((8, GROUP_WIDTH), F32)
        eb_s[0:8, :] = jnp.zeros((8, D_STATE), F32)
        ec_s[0:8, :] = jnp.zeros((8, D_STATE), F32)
        h_s[...] = jnp.zeros_like(h_s)

    def conv(raw_ref, e_s, cw_ref, cb_ref):
        e_s[8:8 + R, :] = raw_ref[...]
        acc = cb_ref[...] + cw_ref[0:1, :] * e_s[5:5 + R, :]
        for k in range(1, CONV_K):
            acc = acc + cw_ref[k:k + 1, :] * e_s[5 + k:5 + k + R, :]
        return _silu(acc)

    xs = conv(x_ref, ex_s, cwx_ref, cbx_ref)
    bm = conv(b_ref, eb_s, cwb_ref, cbb_ref)
    cm = conv(c_ref, ec_s, cwc_ref, cbc_ref)

    @pl.when(c == pl.num_programs(2) - 1)
    def _():
        cvx_ref[...] = ex_s[5 + R:8 + R, :]
        cvb_ref[...] = eb_s[5 + R:8 + R, :]
        cvc_ref[...] = ec_s[5 + R:8 + R, :]

    ex_s[0:8, :] = ex_s[R:R + 8, :]
    eb_s[0:8, :] = eb_s[R:R + 8, :]
    ec_s[0:8, :] = ec_s[R:R + 8, :]

    shift = (LANES - HEADS_PER_GROUP * g) % LANES
    dt_all = _softplus(pltpu.roll(dt_ref[...], shift, 1) + dtb_ref[...])
    dta_all = dt_all * (-jnp.exp(alog_ref[...]))
    row = lax.broadcasted_iota(jnp.int32, (L, L), 0)
    col = lax.broadcasted_iota(jnp.int32, (L, L), 1)
    causal = row >= col
    tril = jnp.where(causal, 1.0, 0.0).astype(BF16)
    lane_a = col < SSM_HEAD_DIM
    row_a = row < SSM_HEAD_DIM
    npair = HEADS_PER_GROUP // 2
    nsub = R // L
    units = [(sc, pair) for sc in range(nsub) for pair in range(npair)]

    acums = [_dot_exact_rhs(tril, dta_all[sc * L:(sc + 1) * L]) for sc in range(nsub)]
    acum_ts = [a.T for a in acums]
    bmbs = [bm[sc * L:(sc + 1) * L].astype(BF16) for sc in range(nsub)]
    cmbs = [cm[sc * L:(sc + 1) * L].astype(BF16) for sc in range(nsub)]
    cbs = [_dot_nt(c, b) for c, b in zip(cmbs, bmbs)]

    def per_head(a, pair):
        return jnp.where(lane_a, a[:, 2 * pair:2 * pair + 1], a[:, 2 * pair + 1:2 * pair + 2])

    xps = {(sc, p): xs[sc * L:(sc + 1) * L, p * LANES:(p + 1) * LANES] for sc, p in units}
    xdts = {(sc, p): xps[sc, p] * per_head(dt_all[sc * L:(sc + 1) * L], p) for sc, p in units}
    ac_ps = {(sc, p): per_head(acums[sc], p) for sc, p in units}
    masked = {}
    for sc, p in units:
        for hd in (2 * p, 2 * p + 1):
            diff = acums[sc][:, hd:hd + 1] - acum_ts[sc][hd:hd + 1, :]
            masked[sc, hd] = (cbs[sc] * jnp.exp(jnp.where(causal, diff, NEG))).astype(BF16)
    y_diags, upds = {}, {}
    for sc, p in units:
        xdtb = xdts[sc, p].astype(BF16)
        y_diags[sc, p] = jnp.where(lane_a, _dot(masked[sc, 2 * p], xdtb), _dot(masked[sc, 2 * p + 1], xdtb))
        last_p = per_head(acums[sc][L - 1:L, :], p)
        xw = xdts[sc, p] * jnp.exp(last_p - ac_ps[sc, p])
        upds[sc, p] = _dot(xw.T.astype(BF16), bmbs[sc])

    for p in range(npair):
        ps = slice(p * LANES, (p + 1) * LANES)
        h = h_s[ps, :]
        for sc in range(nsub):
            rs = slice(sc * L, (sc + 1) * L)
            y_off = _dot_nt(cmbs[sc], h.astype(BF16)) * jnp.exp(ac_ps[sc, p])
            y_ref[rs, ps] = y_diags[sc, p] + y_off + xps[sc, p] * dsk_ref[:, ps]
            last = acums[sc][L - 1:L, :]
            last_r = jnp.where(row_a, last[:, 2 * p:2 * p + 1], last[:, 2 * p + 1:2 * p + 2])
            h = jnp.exp(last_r) * h + upds[sc, p]
        h_s[ps, :] = h
    y_ref[...] = _gate_norm(y_ref[...], z_ref[...], nw_ref[...])

    @pl.when(c == pl.num_programs(2) - 1)
    def _():
        st_ref[...] = h_s[...]


def _ssd_prompt(proj, prm, batch, seq):
    m = proj.shape[0]
    R = SSD_CHUNK * SSD_SUBCHUNKS if seq % (SSD_CHUNK * SSD_SUBCHUNKS) == 0 else SSD_CHUNK
    nc = seq // R
    gw, ds = GROUP_WIDTH, D_STATE
    xcol = D_INNER // gw
    bcol = (2 * D_INNER) // ds
    ccol = bcol + SSM_GROUPS
    dtcol = DT_COL // LANES
    kw = D_INNER // ds
    row = lambda b, g, c: b * nc + c
    in_specs = [
        pl.BlockSpec((R, gw), lambda b, g, c: (row(b, g, c), xcol + g)),
        pl.BlockSpec((R, ds), lambda b, g, c: (row(b, g, c), bcol + g)),
        pl.BlockSpec((R, ds), lambda b, g, c: (row(b, g, c), ccol + g)),
        pl.BlockSpec((R, gw), lambda b, g, c: (row(b, g, c), g)),
        pl.BlockSpec((R, LANES), lambda b, g, c: (row(b, g, c), dtcol)),
        pl.BlockSpec((CONV_K, gw), lambda b, g, c: (0, g)),
        pl.BlockSpec((CONV_K, ds), lambda b, g, c: (0, kw + g)),
        pl.BlockSpec((CONV_K, ds), lambda b, g, c: (0, kw + SSM_GROUPS + g)),
        pl.BlockSpec((1, gw), lambda b, g, c: (0, g)),
        pl.BlockSpec((1, ds), lambda b, g, c: (0, kw + g)),
        pl.BlockSpec((1, ds), lambda b, g, c: (0, kw + SSM_GROUPS + g)),
        pl.BlockSpec((None, 1, LANES), lambda b, g, c: (g, 0, 0)),
        pl.BlockSpec((None, 1, LANES), lambda b, g, c: (g, 0, 0)),
        pl.BlockSpec((1, gw), lambda b, g, c: (0, g)),
        pl.BlockSpec((1, gw), lambda b, g, c: (0, g)),
    ]
    out_specs = [
        pl.BlockSpec((R, gw), lambda b, g, c: (row(b, g, c), g)),
        pl.BlockSpec((None, gw, ds), lambda b, g, c: (b, g, 0)),
        pl.BlockSpec((None, CONV_K - 1, gw), lambda b, g, c: (b, 0, g)),
        pl.BlockSpec((None, CONV_K - 1, ds), lambda b, g, c: (b, 0, g)),
        pl.BlockSpec((None, CONV_K - 1, ds), lambda b, g, c: (b, 0, g)),
    ]
    out_shape = [
        jax.ShapeDtypeStruct((m, D_INNER), F32),
        jax.ShapeDtypeStruct((batch, SSM_HEADS * SSM_HEAD_DIM, ds), F32),
        jax.ShapeDtypeStruct((batch, CONV_K - 1, D_INNER), F32),
        jax.ShapeDtypeStruct((batch, CONV_K - 1, SSM_GROUPS * ds), F32),
        jax.ShapeDtypeStruct((batch, CONV_K - 1, SSM_GROUPS * ds), F32),
    ]
    y, st, cvx, cvb, cvc = pl.pallas_call(
        _ssd_kernel,
        grid=(batch, SSM_GROUPS, nc),
        in_specs=in_specs,
        out_specs=out_specs,
        out_shape=out_shape,
        scratch_shapes=[pltpu.VMEM((R + 8, gw), F32), pltpu.VMEM((R + 8, ds), F32),
                        pltpu.VMEM((R + 8, ds), F32), pltpu.VMEM((gw, ds), F32)],
        compiler_params=_cparams(("arbitrary", "arbitrary", "arbitrary")),
        name="ssd_prompt",
    )(proj, proj, proj, proj, proj,
      prm["conv_w"], prm["conv_w"], prm["conv_w"], prm["conv_b"], prm["conv_b"], prm["conv_b"],
      prm["dt_bias_g"], prm["a_log_g"], prm["d_exp"], prm["norm_w"])
    conv = jnp.concatenate([cvx, cvb, cvc], axis=-1)
    return y, st.reshape(batch, SSM_HEADS, SSM_HEAD_DIM, ds), conv


def _ssd_sample_kernel(x_ref, b_ref, c_ref, z_ref, dt_ref, cvx_ref, cvb_ref, cvc_ref, h_ref,
                       cwx_ref, cwb_ref, cwc_ref, cbx_ref, cbb_ref, cbc_ref,
                       dtb_ref, alog_ref, dsk_ref, nw_ref, e4_ref,
                       y_ref, ho_ref, nvx_ref, nvb_ref, nvc_ref, y_s):
    g = pl.program_id(0)
    bd = x_ref.shape[0]

    def conv(raw_ref, st_ref, new_ref, cw_ref, cb_ref):
        raw = raw_ref[...]
        acc = cb_ref[...] + cw_ref[CONV_K - 1:CONV_K, :] * raw
        for k in range(CONV_K - 1):
            acc = acc + cw_ref[k:k + 1, :] * st_ref[k]
        for k in range(CONV_K - 2):
            new_ref[k] = st_ref[k + 1]
        new_ref[CONV_K - 2] = raw
        return _silu(acc)

    xs = conv(x_ref, cvx_ref, nvx_ref, cwx_ref, cbx_ref)
    bm = conv(b_ref, cvb_ref, nvb_ref, cwb_ref, cbb_ref)
    cm = conv(c_ref, cvc_ref, nvc_ref, cwc_ref, cbc_ref)

    shift = (LANES - HEADS_PER_GROUP * g) % LANES
    dt = _softplus(pltpu.roll(dt_ref[...], shift, 1) + dtb_ref[...])
    da = jnp.exp(dt * (-jnp.exp(alog_ref[...])))
    e4 = e4_ref[...]
    dt_e = _dot_exact(dt, e4)
    da_e = _dot_exact(da, e4)
    xdt = xs * dt_e
    pad = jnp.zeros((LANES - bd, GROUP_WIDTH), F32)
    xdt_t = jnp.concatenate([xdt, pad], axis=0).T
    da_t = jnp.concatenate([da_e, pad], axis=0).T
    cmb = cm.astype(BF16)

    for b in range(bd):
        h_new = da_t[:, b:b + 1] * h_ref[b] + xdt_t[:, b:b + 1] * bm[b:b + 1, :]
        ho_ref[b] = h_new
        yb = _dot_nt(jnp.broadcast_to(cmb[b:b + 1, :], (8, D_STATE)), h_new.astype(BF16))
        y_s[b:b + 1, :] = yb[0:1]

    y = y_s[...] + xs * dsk_ref[...]
    y_ref[...] = _gate_norm(y, z_ref[...], nw_ref[...])


def _ssd_sample(proj, conv_state, h0, prm, e4):
    bd = proj.shape[0]
    gw, ds = GROUP_WIDTH, D_STATE
    xcol = D_INNER // gw
    bcol = (2 * D_INNER) // ds
    ccol = bcol + SSM_GROUPS
    dtcol = DT_COL // LANES
    cst = jnp.transpose(conv_state, (1, 0, 2))
    h0v = h0.reshape(bd, SSM_HEADS * SSM_HEAD_DIM, ds)
    kw = D_INNER // ds
    in_specs = [
        pl.BlockSpec((bd, gw), lambda g: (0, xcol + g)),
        pl.BlockSpec((bd, ds), lambda g: (0, bcol + g)),
        pl.BlockSpec((bd, ds), lambda g: (0, ccol + g)),
        pl.BlockSpec((bd, gw), lambda g: (0, g)),
        pl.BlockSpec((bd, LANES), lambda g: (0, dtcol)),
        pl.BlockSpec((CONV_K - 1, bd, gw), lambda g: (0, 0, g)),
        pl.BlockSpec((CONV_K - 1, bd, ds), lambda g: (0, 0, kw + g)),
        pl.BlockSpec((CONV_K - 1, bd, ds), lambda g: (0, 0, kw + SSM_GROUPS + g)),
        pl.BlockSpec((bd, gw, ds), lambda g: (0, g, 0)),
        pl.BlockSpec((CONV_K, gw), lambda g: (0, g)),
        pl.BlockSpec((CONV_K, ds), lambda g: (0, kw + g)),
        pl.BlockSpec((CONV_K, ds), lambda g: (0, kw + SSM_GROUPS + g)),
        pl.BlockSpec((1, gw), lambda g: (0, g)),
        pl.BlockSpec((1, ds), lambda g: (0, kw + g)),
        pl.BlockSpec((1, ds), lambda g: (0, kw + SSM_GROUPS + g)),
        pl.BlockSpec((None, 1, LANES), lambda g: (g, 0, 0)),
        pl.BlockSpec((None, 1, LANES), lambda g: (g, 0, 0)),
        pl.BlockSpec((1, gw), lambda g: (0, g)),
        pl.BlockSpec((1, gw), lambda g: (0, g)),
        pl.BlockSpec((LANES, gw), lambda g: (0, 0)),
    ]
    out_specs = [
        pl.BlockSpec((bd, gw), lambda g: (0, g)),
        pl.BlockSpec((bd, gw, ds), lambda g: (0, g, 0)),
        pl.BlockSpec((CONV_K - 1, bd, gw), lambda g: (0, 0, g)),
        pl.BlockSpec((CONV_K - 1, bd, ds), lambda g: (0, 0, g)),
        pl.BlockSpec((CONV_K - 1, bd, ds), lambda g: (0, 0, g)),
    ]
    out_shape = [
        jax.ShapeDtypeStruct((bd, D_INNER), F32),
        jax.ShapeDtypeStruct((bd, SSM_HEADS * SSM_HEAD_DIM, ds), F32),
        jax.ShapeDtypeStruct((CONV_K - 1, bd, D_INNER), F32),
        jax.ShapeDtypeStruct((CONV_K - 1, bd, SSM_GROUPS * ds), F32),
        jax.ShapeDtypeStruct((CONV_K - 1, bd, SSM_GROUPS * ds), F32),
    ]
    y, hn, nvx, nvb, nvc = pl.pallas_call(
        _ssd_sample_kernel,
        grid=(SSM_GROUPS,),
        in_specs=in_specs,
        out_specs=out_specs,
        out_shape=out_shape,
        scratch_shapes=[pltpu.VMEM((bd, gw), F32)],
        compiler_params=_cparams(("arbitrary",)),
        name="ssd_sample",
    )(proj, proj, proj, proj, proj, cst, cst, cst, h0v,
      prm["conv_w"], prm["conv_w"], prm["conv_w"], prm["conv_b"], prm["conv_b"], prm["conv_b"],
      prm["dt_bias_g"], prm["a_log_g"], prm["d_exp"], prm["norm_w"], e4)
    conv = jnp.transpose(jnp.concatenate([nvx, nvb, nvc], axis=-1), (1, 0, 2))
    return y, hn.reshape(bd, SSM_HEADS, SSM_HEAD_DIM, ds), conv


def _group_lanes(v):
    v = v.reshape(SSM_GROUPS, 1, HEADS_PER_GROUP).astype(F32)
    return jnp.pad(v, ((0, 0), (0, 0), (0, LANES - HEADS_PER_GROUP)))


def kernel(x_prompt, x_sample, cache_kv_w128, cache_kv_w512, cache_kv_w2048, state_ssm, state_conv,
           attn_w_in, attn_w_out, ssm_w_in, ssm_conv_w, ssm_conv_b, ssm_dt_bias, ssm_a_log, ssm_d,
           ssm_norm_w, ssm_w_out, mlp_w1, mlp_w2, ln_mix_g, ln_mix_b, ln_ffn_g, ln_ffn_b):
    caches = (cache_kv_w128, cache_kv_w512, cache_kv_w2048)
    bp, seq, d = x_prompt.shape
    bd = x_sample.shape[0]
    mp = bp * seq
    xp = x_prompt.reshape(mp, d)
    xs = x_sample.reshape(bd, d)

    tm_p = 1024 if seq % 1024 == 0 else seq
    tm_qkv = 512 if seq % 512 == 0 else seq
    tm_ssm = 256 if seq % 256 == 0 else seq
    rope_p = _rope_tables(jnp.arange(seq, dtype=jnp.int32))
    rope_s = _rope_tables(jnp.full((bd,), PAST_LEN, dtype=jnp.int32))
    e4 = ((jnp.arange(GROUP_WIDTH) // SSM_HEAD_DIM)[None, :] == jnp.arange(LANES)[:, None]).astype(BF16)

    caches_t = [jnp.transpose(c, (0, 1, 3, 4, 5, 2)) for c in caches]
    updated = None
    kv_p = [[] for _ in ATTN_GROUPS]
    ssm_p, conv_p, ssm_s, conv_s = [], [], [], []

    for i in range(DEPTH):
        j = i // 2
        if i % 2 == 0:
            w_in = attn_w_in[j].astype(BF16)
            w_out = attn_w_out[j].astype(BF16)
            qkv_p = _project(xp, w_in, tm_qkv, QKV_WIDTH, rope=rope_p, table_blocks=seq // tm_qkv)
            qkv_s = _project(xs, w_in, bd, QKV_WIDTH, rope=rope_s)
            hp, *tails = _attention_prompt(qkv_p, bp, seq)
            hs, updated = _cache_attention(qkv_s, caches_t, updated, j)
            for gi in range(N_GROUPS):
                kv_p[gi].append(tails[gi])
        else:
            w_in = jnp.pad(ssm_w_in[j], ((0, 0), (0, SSM_IN_PAD - ssm_w_in.shape[2]))).astype(BF16)
            w_out = ssm_w_out[j].astype(BF16)
            prm = {
                "conv_w": ssm_conv_w[j],
                "conv_b": ssm_conv_b[j].reshape(1, CONV_DIM),
                "dt_bias_g": _group_lanes(ssm_dt_bias[j]),
                "a_log_g": _group_lanes(ssm_a_log[j]),
                "d_exp": jnp.repeat(ssm_d[j].astype(F32), SSM_HEAD_DIM).reshape(1, D_INNER),
                "norm_w": ssm_norm_w[j].reshape(1, D_INNER),
            }
            proj_p = _project(xp, w_in, tm_ssm, SSM_IN_PAD)
            proj_s = _project(xs, w_in, bd, SSM_IN_PAD)
            hp, st_p, cv_p = _ssd_prompt(proj_p, prm, bp, seq)
            hs, st_s, cv_s = _ssd_sample(proj_s, state_conv[j], state_ssm[j], prm, e4)
            ssm_p.append(st_p)
            conv_p.append(cv_p)
            ssm_s.append(st_s)
            conv_s.append(cv_s)
        xp = _out_ln(hp, w_out, xp, ln_mix_g[i], ln_mix_b[i], 512)
        xs = _out_ln(hs, w_out, xs, ln_mix_g[i], ln_mix_b[i], bd)
        w1 = mlp_w1[i].astype(BF16)
        w2 = mlp_w2[i].astype(BF16)
        xp = _mlp(xp, w1, w2, ln_ffn_g[i], ln_ffn_b[i], tm_p)
        xs = _mlp(xs, w1, w2, ln_ffn_g[i], ln_ffn_b[i], bd)

    back = (0, 1, 5, 2, 3, 4)
    kv_p_out = [jnp.transpose(jnp.stack(t), back) for t in kv_p]
    kv_s_out = [jnp.transpose(u, back) for u in updated]
    return (xp.reshape(bp, seq, d), xs.reshape(bd, 1, d),
            kv_p_out[0], kv_p_out[1], kv_p_out[2], jnp.stack(ssm_p), jnp.stack(conv_p),
            kv_s_out[0], kv_s_out[1], kv_s_out[2], jnp.stack(ssm_s), jnp.stack(conv_s))
```

```python
import functools

import jax
import jax.numpy as jnp
from jax import lax
from jax.experimental import pallas as pl
from jax.experimental.pallas import tpu as pltpu

F32 = jnp.float32
BF16 = jnp.bfloat16

D_MODEL = 1024
DEPTH = 4
PAST_LEN = 8192
ATTN_GROUPS = ((128, 1), (512, 4), (2048, 16))
ATTN_UNROLL = ((8, 1), (8, 1), (2, 2))
N_GROUPS = len(ATTN_GROUPS)
NSTEP = 128
ATTN_HEADS = 16
HEAD_DIM = 64
ATTN_WIDTH = ATTN_HEADS * HEAD_DIM
QKV_WIDTH = 3 * ATTN_WIDTH
CACHE_HEADS = 2
ROPE_THETA = 10000.0
SCALE = HEAD_DIM ** -0.5
LOG2E = 1.4426950408889634

D_INNER = 2048
SSM_HEAD_DIM = 64
SSM_HEADS = 32
SSM_GROUPS = 8
HEADS_PER_GROUP = 4
GROUP_WIDTH = HEADS_PER_GROUP * SSM_HEAD_DIM
D_STATE = 128
CONV_K = 4
CONV_DIM = D_INNER + 2 * SSM_GROUPS * D_STATE
DT_COL = D_INNER + CONV_DIM
SSM_IN_PAD = DT_COL + 128
SSD_CHUNK = 128
SSD_SUBCHUNKS = 4

D_FF = 4096
ALPHA = (2 * DEPTH) ** 0.25
LN_EPS = 1e-5
RMS_EPS = 1e-5

LANES = 128
NEG = -1e30
VMEM_LIMIT = 56 * 1024 * 1024


def _cparams(sem):
    return pltpu.CompilerParams(dimension_semantics=sem, vmem_limit_bytes=VMEM_LIMIT)


def _dot(a, b):
    return jnp.dot(a, b, preferred_element_type=F32)


def _dot_nt(a, b):
    return lax.dot_general(a, b, (((1,), (1,)), ((), ())), preferred_element_type=F32)


def _split3(x):
    hi = x.astype(BF16)
    r1 = x - hi.astype(F32)
    mid = r1.astype(BF16)
    lo = (r1 - mid.astype(F32)).astype(BF16)
    return hi, mid, lo


def _dot_exact(x, e):
    hi, mid, lo = _split3(x)
    return _dot(hi, e) + _dot(mid, e) + _dot(lo, e)


def _dot_exact_rhs(e, x):
    hi, mid, lo = _split3(x)
    return _dot(e, hi) + _dot(e, mid) + _dot(e, lo)


def _silu(x):
    return x / (1.0 + jnp.exp(-x))


def _softplus(x):
    return jnp.maximum(x, 0.0) + jnp.log(1.0 + jnp.exp(-jnp.abs(x)))


def _layer_norm(y, g, b):
    mu = jnp.mean(y, axis=-1, keepdims=True)
    yc = y - mu
    var = jnp.mean(yc * yc, axis=-1, keepdims=True)
    return yc * lax.rsqrt(var + LN_EPS) * g + b


def _proj_kernel(x_ref, w_ref, o_ref):
    xb = x_ref[...].astype(BF16)
    chunk = 4 * LANES
    n = o_ref.shape[1]
    for lo in range(0, n, chunk):
        hi = min(lo + chunk, n)
        o_ref[:, lo:hi] = _dot(xb, w_ref[:, lo:hi])


def _proj_rope_kernel(x_ref, w_ref, cos_ref, sa_ref, sb_ref, o_ref):
    xb = x_ref[...].astype(BF16)
    chunk = 2 * LANES
    for kind in range(3):
        cos, sa, sb = cos_ref[kind], sa_ref[kind], sb_ref[kind]
        for c in range(ATTN_WIDTH // chunk):
            base = kind * ATTN_WIDTH + c * chunk
            acc = _dot(xb, w_ref[:, base:base + chunk])
            for h in range(chunk // LANES):
                seg = acc[:, h * LANES:(h + 1) * LANES]
                lo = base + h * LANES
                o_ref[:, lo:lo + LANES] = (
                    seg * cos + pltpu.roll(seg, 96, 1) * sa + pltpu.roll(seg, 32, 1) * sb)


def _project(x, w, tm, tn, rope=None, table_blocks=1):
    m, k = x.shape
    n = w.shape[1]
    grid = (n // tn, m // tm)
    in_specs = [pl.BlockSpec((tm, k), lambda j, i: (i, 0)),
                pl.BlockSpec((k, tn), lambda j, i: (0, j))]
    args = [x, w]
    if rope is None:
        body = _proj_kernel
    else:
        body = _proj_rope_kernel
        tspec = pl.BlockSpec((3, tm, LANES), lambda j, i: (0, i % table_blocks, 0))
        in_specs += [tspec, tspec, tspec]
        args += list(rope)
    return pl.pallas_call(
        body,
        grid=grid,
        in_specs=in_specs,
        out_specs=pl.BlockSpec((tm, tn), lambda j, i: (i, j)),
        out_shape=jax.ShapeDtypeStruct((m, n), F32),
        compiler_params=_cparams(("arbitrary", "arbitrary")),
        name="proj_rope" if rope is not None else "proj",
    )(*args)


def _rope_tables(pos, q_scale):
    half = HEAD_DIM // 2
    inv = jnp.power(ROPE_THETA, -jnp.arange(half, dtype=F32) * (2.0 / HEAD_DIM))
    ang = pos.astype(F32)[:, None] * inv[None, :]
    cos = jnp.tile(jnp.cos(ang), (1, LANES // half))
    sin = jnp.tile(jnp.sin(ang), (1, LANES // half))
    first = (jnp.arange(LANES) % HEAD_DIM) < half
    sa = jnp.where(first[None, :], -sin, 0.0)
    sb = jnp.where(first[None, :], 0.0, sin)
    one, zero = jnp.ones_like(cos), jnp.zeros_like(cos)
    return (jnp.stack([cos * q_scale, cos, one]), jnp.stack([sa * q_scale, sa, zero]),
            jnp.stack([sb * q_scale, sb, zero]))


def _attn_kernel(q_ref, k_ref, v_ref, o_ref, kv0_ref, kv1_ref, kv2_ref, m_s, l_s, acc_s, *, seq):
    g = pl.program_id(2)
    lane_a = lax.broadcasted_iota(jnp.int32, (NSTEP, LANES), 1) < HEAD_DIM
    kj = lax.broadcasted_iota(jnp.int32, (NSTEP, 2 * NSTEP), 1)
    dmat = kj - lax.broadcasted_iota(jnp.int32, (NSTEP, 2 * NSTEP), 0)
    bias_full = jnp.where(dmat >= 0, jnp.where(dmat <= NSTEP, 0.0, NEG), NEG).astype(F32)
    bias_own = jnp.where(kj >= NSTEP, bias_full, NEG)
    lane_a2 = lax.broadcasted_iota(jnp.int32, (2 * NSTEP, LANES), 1) < HEAD_DIM
    ones = jnp.ones((2 * NSTEP, LANES), BF16)

    def run_group(dil, first, unroll, res_per_iter):
        nb = seq // dil // NSTEP
        niter = nb // unroll
        span = unroll * NSTEP

        def rows(start, size):
            if dil == 1:
                return pl.ds(pl.multiple_of(start, NSTEP), size)
            return pl.ds(start, size, stride=dil)

        def step(t, carry):
            n0 = (t % niter) * unroll
            bias0 = jnp.where(n0 > 0, bias_full, bias_own)
            zero = jnp.zeros((NSTEP, LANES), BF16)
            chains, q_rows_of = [], []
            for rr in range(res_per_iter):
                r = (t // niter) * res_per_iter + rr
                base = r + dil * NSTEP * n0
                prev = r + dil * NSTEP * jnp.maximum(n0 - 1, 0)
                q_all = q_ref[rows(base, span), :].astype(BF16)
                k_all = k_ref[rows(base, span), :].astype(BF16)
                v_all = v_ref[rows(base, span), :].astype(BF16)
                k_prev = k_ref[rows(prev, NSTEP), :].astype(BF16)
                v_prev = v_ref[rows(prev, NSTEP), :].astype(BF16)
                for u in range(unroll):
                    qb = q_all[u * NSTEP:(u + 1) * NSTEP]
                    if u == 0:
                        kw = jnp.concatenate([k_prev, k_all[0:NSTEP]], axis=0)
                        vw = jnp.concatenate([v_prev, v_all[0:NSTEP]], axis=0)
                        bias = bias0
                    else:
                        kw = k_all[(u - 1) * NSTEP:(u + 1) * NSTEP]
                        vw = v_all[(u - 1) * NSTEP:(u + 1) * NSTEP]
                        bias = bias_full
                    chains.append((jnp.where(lane_a, qb, zero), kw, jnp.where(lane_a2, vw, ones), bias))
                    chains.append((jnp.where(lane_a, zero, qb), kw, jnp.where(lane_a2, ones, vw), bias))
                    q_rows_of.append(rows(base + dil * NSTEP * u, NSTEP))
            scores = [_dot_nt(qh, kw) + bias for qh, kw, _, bias in chains]
            maxes = [jnp.max(s, axis=1, keepdims=True) for s in scores]
            probs = [jnp.exp2(s - mx).astype(BF16) for s, mx in zip(scores, maxes)]
            pvs = [_dot(p, ch[2]) for p, ch in zip(probs, chains)]
            for u, q_rows in enumerate(q_rows_of):
                o_a, o_b = pvs[2 * u], pvs[2 * u + 1]
                m_blk = jnp.where(lane_a, maxes[2 * u], maxes[2 * u + 1])
                o_blk = jnp.where(lane_a, o_a, o_b)
                l_blk = pltpu.roll(jnp.where(lane_a, o_b, o_a), HEAD_DIM, 1)
                if first:
                    m_s[q_rows, :] = m_blk
                    l_s[q_rows, :] = l_blk
                    acc_s[q_rows, :] = o_blk
                else:
                    m_old = m_s[q_rows, :]
                    m_new = jnp.maximum(m_old, m_blk)
                    a_old = jnp.exp2(m_old - m_new)
                    a_blk = jnp.exp2(m_blk - m_new)
                    m_s[q_rows, :] = m_new
                    l_s[q_rows, :] = a_old * l_s[q_rows, :] + a_blk * l_blk
                    acc_s[q_rows, :] = a_old * acc_s[q_rows, :] + a_blk * o_blk
            return carry

        lax.fori_loop(0, (dil // res_per_iter) * niter, step, 0)

    def write_tail(kv_ref, keep):
        for j in range(keep // LANES):
            r0 = seq - keep + j * LANES
            kt = k_ref[r0:r0 + LANES, :].T
            vt = v_ref[r0:r0 + LANES, :].T
            cols = slice(j * LANES, (j + 1) * LANES)
            kv_ref[0, 0, :, cols] = kt[0:HEAD_DIM]
            kv_ref[0, 1, :, cols] = kt[HEAD_DIM:2 * HEAD_DIM]
            kv_ref[1, 0, :, cols] = vt[0:HEAD_DIM]
            kv_ref[1, 1, :, cols] = vt[HEAD_DIM:2 * HEAD_DIM]

    for gi, ((win, dil), kv_ref) in enumerate(zip(ATTN_GROUPS, (kv0_ref, kv1_ref, kv2_ref))):
        @pl.when(g == N_GROUPS - 1 - gi)
        def _(dil=dil, gi=gi, win=win, kv_ref=kv_ref):
            run_group(dil, gi == N_GROUPS - 1, *ATTN_UNROLL[gi])
            write_tail(kv_ref, min(win, seq))

    @pl.when(g == N_GROUPS - 1)
    def _():
        o_ref[...] = (acc_s[...] / l_s[...]).astype(o_ref.dtype)


def _attention_prompt(qkv, batch, seq):
    m = qkv.shape[0]
    cpg = QKV_WIDTH // LANES
    cpk = ATTN_WIDTH // LANES
    last = N_GROUPS - 1
    blk = (seq, LANES)
    keeps = [min(win, seq) for win, _ in ATTN_GROUPS]
    return pl.pallas_call(
        functools.partial(_attn_kernel, seq=seq),
        grid=(batch, cpk, N_GROUPS),
        in_specs=[pl.BlockSpec(blk, lambda b, hp, g: (b, (last - g) * cpg + hp)),
                  pl.BlockSpec(blk, lambda b, hp, g: (b, (last - g) * cpg + cpk + hp)),
                  pl.BlockSpec(blk, lambda b, hp, g: (b, (last - g) * cpg + 2 * cpk + hp))],
        out_specs=[pl.BlockSpec(blk, lambda b, hp, g: (b, hp))]
                  + [pl.BlockSpec((None, 2, 2, HEAD_DIM, keep), lambda b, hp, g: (b, 0, hp, 0, 0))
                     for keep in keeps],
        out_shape=[jax.ShapeDtypeStruct((m, ATTN_WIDTH), BF16)]
                  + [jax.ShapeDtypeStruct((batch, 2, ATTN_HEADS, HEAD_DIM, keep), F32) for keep in keeps],
        scratch_shapes=[pltpu.VMEM(blk, F32)] * 3,
        compiler_params=_cparams(("arbitrary", "arbitrary", "arbitrary")),
        name="attn_prompt",
    )(qkv, qkv, qkv)


def _eye64():
    return (lax.broadcasted_iota(jnp.int32, (HEAD_DIM, HEAD_DIM), 0)
            == lax.broadcasted_iota(jnp.int32, (HEAD_DIM, HEAD_DIM), 1))


def _row_to_col(row, eye):
    return jnp.sum(jnp.where(eye, jnp.broadcast_to(row, (HEAD_DIM, HEAD_DIM)), 0.0), axis=1, keepdims=True)


def _col_to_row(col, eye):
    return jnp.sum(jnp.where(eye, jnp.broadcast_to(col, (HEAD_DIM, HEAD_DIM)), 0.0), axis=0, keepdims=True)


def _sample_group(qkv_ref, gi, hh, kt, vt, dil, eye):
    q = _row_to_col(qkv_ref[3 * gi, hh], eye)
    kn = _row_to_col(qkv_ref[3 * gi + 1, hh], eye)
    vn = _row_to_col(qkv_ref[3 * gi + 2, hh], eye)
    s = jnp.sum(q * kt, axis=0, keepdims=True)
    if dil > 1:
        pos = lax.broadcasted_iota(jnp.int32, s.shape, 1)
        s = jnp.where((pos & (dil - 1)) == 0, s, NEG)
    sn = jnp.sum(q * kn, axis=0, keepdims=True)
    mx = jnp.maximum(jnp.max(s, axis=1, keepdims=True), sn)
    p = jnp.exp(s - mx)
    pn = jnp.exp(sn - mx)
    den = jnp.sum(p, axis=1, keepdims=True) + pn
    o = (jnp.sum(p * vt, axis=1, keepdims=True) + pn * vn) / den
    return o, mx + jnp.log(den), kn, vn


def _merge_groups(outs, lses, eye):
    top = jnp.maximum(jnp.maximum(lses[0], lses[1]), lses[2])
    ws = [jnp.exp(l - top) for l in lses]
    acc = ws[0] * outs[0] + ws[1] * outs[1] + ws[2] * outs[2]
    return _col_to_row(acc / (ws[0] + ws[1] + ws[2]), eye)


def _shift_left(x):
    return pltpu.roll(x, x.shape[1] - 1, 1)


def _shift_in(x, new_col):
    lane = lax.broadcasted_iota(jnp.int32, x.shape, 1)
    return jnp.where(lane == x.shape[1] - 1, new_col, _shift_left(x))


def _cache_attn_first_kernel(qkv_ref, c0_ref, c1_ref, c2_ref, o_ref, n0_ref, n1_ref, n2_ref):
    eye = _eye64()
    for hh in range(CACHE_HEADS):
        outs, lses = [], []
        for gi, (c_ref, n_ref) in enumerate(zip((c0_ref, c1_ref, c2_ref), (n0_ref, n1_ref, n2_ref))):
            kt, vt = c_ref[0, 0, hh], c_ref[0, 1, hh]
            o, lse, kn, vn = _sample_group(qkv_ref, gi, hh, kt, vt, ATTN_GROUPS[gi][1], eye)
            outs.append(o)
            lses.append(lse)
            n_ref[0, 0, hh] = _shift_in(kt, kn)
            n_ref[0, 1, hh] = _shift_in(vt, vn)
            for layer in range(1, c_ref.shape[0]):
                n_ref[layer, 0, hh] = _shift_left(c_ref[layer, 0, hh])
                n_ref[layer, 1, hh] = _shift_left(c_ref[layer, 1, hh])
        o_ref[hh] = _merge_groups(outs, lses, eye)


def _cache_attn_later_kernel(qkv_ref, c0_ref, c1_ref, c2_ref, a0_ref, a1_ref, a2_ref,
                             o_ref, n0_ref, n1_ref, n2_ref):
    del a0_ref, a1_ref, a2_ref
    eye = _eye64()
    for hh in range(CACHE_HEADS):
        outs, lses = [], []
        for gi, (c_ref, n_ref) in enumerate(zip((c0_ref, c1_ref, c2_ref), (n0_ref, n1_ref, n2_ref))):
            kt, vt = c_ref[0, hh], c_ref[1, hh]
            o, lse, kn, vn = _sample_group(qkv_ref, gi, hh, kt, vt, ATTN_GROUPS[gi][1], eye)
            outs.append(o)
            lses.append(lse)
            win = kt.shape[1]
            n_ref[0, hh] = _shift_in(kt[:, win - LANES:], kn)
            n_ref[1, hh] = _shift_in(vt[:, win - LANES:], vn)
        o_ref[hh] = _merge_groups(outs, lses, eye)


def _cache_attention(qkv, caches_t, updated, layer):
    bd = qkv.shape[0]
    q5 = qkv.reshape(bd, 3 * N_GROUPS, ATTN_HEADS, 1, HEAD_DIM)
    hps = CACHE_HEADS
    grid = (bd, ATTN_HEADS // hps)
    q_spec = pl.BlockSpec((None, 3 * N_GROUPS, hps, 1, HEAD_DIM), lambda b, h: (b, 0, h, 0, 0))
    o_spec = pl.BlockSpec((None, hps, 1, HEAD_DIM), lambda b, h: (b, h, 0, 0))
    o_shape = jax.ShapeDtypeStruct((bd, ATTN_HEADS, 1, HEAD_DIM), F32)
    n_shapes = [jax.ShapeDtypeStruct(c.shape, F32) for c in caches_t]
    if layer == 0:
        c_specs = [pl.BlockSpec((c.shape[0], None, 2, hps, HEAD_DIM, c.shape[5]),
                                lambda b, h: (0, b, 0, h, 0, 0)) for c in caches_t]
        res = pl.pallas_call(
            _cache_attn_first_kernel,
            grid=grid,
            in_specs=[q_spec] + c_specs,
            out_specs=[o_spec] + c_specs,
            out_shape=[o_shape] + n_shapes,
            compiler_params=_cparams(("arbitrary", "arbitrary")),
            name="cache_attn_first",
        )(q5, *caches_t)
    else:
        c_specs = [pl.BlockSpec((None, None, 2, hps, HEAD_DIM, c.shape[5]),
                                lambda b, h, layer=layer: (layer, b, 0, h, 0, 0)) for c in caches_t]
        n_specs = [pl.BlockSpec((None, None, 2, hps, HEAD_DIM, LANES),
                                lambda b, h, layer=layer, last=c.shape[5] // LANES - 1: (layer, b, 0, h, 0, last))
                   for c in caches_t]
        any_spec = pl.BlockSpec(memory_space=pl.ANY)
        res = pl.pallas_call(
            _cache_attn_later_kernel,
            grid=grid,
            in_specs=[q_spec] + c_specs + [any_spec] * N_GROUPS,
            out_specs=[o_spec] + n_specs,
            out_shape=[o_shape] + n_shapes,
            input_output_aliases={1 + N_GROUPS + gi: 1 + gi for gi in range(N_GROUPS)},
            compiler_params=_cparams(("arbitrary", "arbitrary")),
            name="cache_attn_later",
        )(q5, *caches_t, *updated)
    return res[0].reshape(bd, ATTN_WIDTH), list(res[1:])


def _out_ln_kernel(a_ref, w_ref, x_ref, g_ref, b_ref, o_ref):
    tm = o_ref.shape[0]
    sub = min(tm, LANES)
    for r in range(0, tm, sub):
        y = _dot(a_ref[r:r + sub, :].astype(BF16), w_ref[...]) + ALPHA * x_ref[r:r + sub, :]
        o_ref[r:r + sub, :] = _layer_norm(y, g_ref[...], b_ref[...])


def _out_ln(a, w, x, g, b, tm):
    m, k = a.shape
    d = w.shape[1]
    return pl.pallas_call(
        _out_ln_kernel,
        grid=(m // tm,),
        in_specs=[pl.BlockSpec((tm, k), lambda i: (i, 0)),
                  pl.BlockSpec((k, d), lambda i: (0, 0)),
                  pl.BlockSpec((tm, d), lambda i: (i, 0)),
                  pl.BlockSpec((1, d), lambda i: (0, 0)),
                  pl.BlockSpec((1, d), lambda i: (0, 0))],
        out_specs=pl.BlockSpec((tm, d), lambda i: (i, 0)),
        out_shape=jax.ShapeDtypeStruct((m, d), F32),
        compiler_params=_cparams(("arbitrary",)),
        name="out_ln",
    )(a, w, x, g.reshape(1, d), b.reshape(1, d))


def _mlp_kernel(x_ref, w1_ref, w2_ref, g_ref, b_ref, o_ref, xb_s, acc_s):
    f = pl.program_id(1)

    @pl.when(f == 0)
    def _():
        x = x_ref[...]
        xb_s[...] = x.astype(BF16)
        acc_s[...] = ALPHA * x

    h = jnp.maximum(_dot(xb_s[...], w1_ref[...]), 0.0)
    acc_s[...] += _dot((h * h).astype(BF16), w2_ref[...])

    @pl.when(f == pl.num_programs(1) - 1)
    def _():
        o_ref[...] = _layer_norm(acc_s[...], g_ref[...], b_ref[...])


def _mlp(x, w1, w2, g, b, tm, tf=1024):
    m, d = x.shape
    ff = w1.shape[1]
    return pl.pallas_call(
        _mlp_kernel,
        grid=(m // tm, ff // tf),
        in_specs=[pl.BlockSpec((tm, d), lambda i, f: (i, 0)),
                  pl.BlockSpec((d, tf), lambda i, f: (0, f)),
                  pl.BlockSpec((tf, d), lambda i, f: (f, 0)),
                  pl.BlockSpec((1, d), lambda i, f: (0, 0)),
                  pl.BlockSpec((1, d), lambda i, f: (0, 0))],
        out_specs=pl.BlockSpec((tm, d), lambda i, f: (i, 0)),
        out_shape=jax.ShapeDtypeStruct((m, d), F32),
        scratch_shapes=[pltpu.VMEM((tm, d), BF16), pltpu.VMEM((tm, d), F32)],
        compiler_params=_cparams(("arbitrary", "arbitrary")),
        name="mlp",
    )(x, w1, w2, g.reshape(1, d), b.reshape(1, d))


def _gate_norm(y, z, nw):
    y = y * _silu(z)
    ms = jnp.mean(y * y, axis=-1, keepdims=True)
    return y * lax.rsqrt(ms + RMS_EPS) * nw


def _ssd_kernel(x_ref, b_ref, c_ref, z_ref, dt_ref,
                cwx_ref, cwb_ref, cwc_ref, cbx_ref, cbb_ref, cbc_ref,
                dtb_ref, alog_ref, dsk_ref, nw_ref,
                y_ref, st_ref, cvx_ref, cvb_ref, cvc_ref,
                ex_s, eb_s, ec_s, h_s):
    g = pl.program_id(1)
    c = pl.program_id(2)
    L = SSD_CHUNK
    R = x_ref.shape[0]

    @pl.when(c == 0)
    def _():
        ex_s[0:8, :] = jnp.zeros((8, GROUP_WIDTH), F32)
        eb_s[0:8, :] = jnp.zeros((8, D_STATE), F32)
        ec_s[0:8, :] = jnp.zeros((8, D_STATE), F32)
        h_s[...] = jnp.zeros_like(h_s)

    def conv(raw_ref, e_s, cw_ref, cb_ref):
        e_s[8:8 + R, :] = raw_ref[...]
        acc = cb_ref[...] + cw_ref[0:1, :] * e_s[5:5 + R, :]
        for k in range(1, CONV_K):
            acc = acc + cw_ref[k:k + 1, :] * e_s[5 + k:5 + k + R, :]
        return _silu(acc)

    xs = conv(x_ref, ex_s, cwx_ref, cbx_ref)
    bm = conv(b_ref, eb_s, cwb_ref, cbb_ref)
    cm = conv(c_ref, ec_s, cwc_ref, cbc_ref)

    @pl.when(c == pl.num_programs(2) - 1)
    def _():
        cvx_ref[...] = ex_s[5 + R:8 + R, :]
        cvb_ref[...] = eb_s[5 + R:8 + R, :]
        cvc_ref[...] = ec_s[5 + R:8 + R, :]

    ex_s[0:8, :] = ex_s[R:R + 8, :]
    eb_s[0:8, :] = eb_s[R:R + 8, :]
    ec_s[0:8, :] = ec_s[R:R + 8, :]

    shift = (LANES - HEADS_PER_GROUP * g) % LANES
    dt_all = _softplus(pltpu.roll(dt_ref[...], shift, 1) + dtb_ref[...])
    dta_all = dt_all * (-jnp.exp(alog_ref[...]))
    row = lax.broadcasted_iota(jnp.int32, (L, L), 0)
    col = lax.broadcasted_iota(jnp.int32, (L, L), 1)
    causal = row >= col
    tril = jnp.where(causal, 1.0, 0.0).astype(BF16)
    lane_a = col < SSM_HEAD_DIM
    row_a = row < SSM_HEAD_DIM
    npair = HEADS_PER_GROUP // 2
    nsub = R // L
    units = [(sc, pair) for sc in range(nsub) for pair in range(npair)]

    acums = [_dot_exact_rhs(tril, dta_all[sc * L:(sc + 1) * L]) * LOG2E for sc in range(nsub)]
    acum_ts = [a.T for a in acums]
    bmbs = [bm[sc * L:(sc + 1) * L].astype(BF16) for sc in range(nsub)]
    cmbs = [cm[sc * L:(sc + 1) * L].astype(BF16) for sc in range(nsub)]
    cbs = [_dot_nt(c, b) for c, b in zip(cmbs, bmbs)]

    def per_head(a, pair):
        return jnp.where(lane_a, a[:, 2 * pair:2 * pair + 1], a[:, 2 * pair + 1:2 * pair + 2])

    xps = {(sc, p): xs[sc * L:(sc + 1) * L, p * LANES:(p + 1) * LANES] for sc, p in units}
    xdts = {(sc, p): xps[sc, p] * per_head(dt_all[sc * L:(sc + 1) * L], p) for sc, p in units}
    ac_ps = {(sc, p): per_head(acums[sc], p) for sc, p in units}
    masked = {}
    for sc, p in units:
        for hd in (2 * p, 2 * p + 1):
            diff = acums[sc][:, hd:hd + 1] - acum_ts[sc][hd:hd + 1, :]
            masked[sc, hd] = (cbs[sc] * jnp.exp2(jnp.where(causal, diff, NEG))).astype(BF16)
    y_diags, upds = {}, {}
    for sc, p in units:
        xdtb = xdts[sc, p].astype(BF16)
        y_diags[sc, p] = jnp.where(lane_a, _dot(masked[sc, 2 * p], xdtb), _dot(masked[sc, 2 * p + 1], xdtb))
        last_p = per_head(acums[sc][L - 1:L, :], p)
        xw = xdts[sc, p] * jnp.exp2(last_p - ac_ps[sc, p])
        upds[sc, p] = _dot(xw.T.astype(BF16), bmbs[sc])

    for p in range(npair):
        ps = slice(p * LANES, (p + 1) * LANES)
        h = h_s[ps, :]
        for sc in range(nsub):
            rs = slice(sc * L, (sc + 1) * L)
            y_off = _dot_nt(cmbs[sc], h.astype(BF16)) * jnp.exp2(ac_ps[sc, p])
            y_ref[rs, ps] = y_diags[sc, p] + y_off + xps[sc, p] * dsk_ref[:, ps]
            last = acums[sc][L - 1:L, :]
            last_r = jnp.where(row_a, last[:, 2 * p:2 * p + 1], last[:, 2 * p + 1:2 * p + 2])
            h = jnp.exp2(last_r) * h + upds[sc, p]
        h_s[ps, :] = h
    y_ref[...] = _gate_norm(y_ref[...], z_ref[...], nw_ref[...])

    @pl.when(c == pl.num_programs(2) - 1)
    def _():
        st_ref[...] = h_s[...]


def _ssd_prompt(proj, prm, batch, seq):
    m = proj.shape[0]
    R = SSD_CHUNK * SSD_SUBCHUNKS if seq % (SSD_CHUNK * SSD_SUBCHUNKS) == 0 else SSD_CHUNK
    nc = seq // R
    gw, ds = GROUP_WIDTH, D_STATE
    xcol = D_INNER // gw
    bcol = (2 * D_INNER) // ds
    ccol = bcol + SSM_GROUPS
    dtcol = DT_COL // LANES
    kw = D_INNER // ds
    row = lambda b, g, c: b * nc + c
    in_specs = [
        pl.BlockSpec((R, gw), lambda b, g, c: (row(b, g, c), xcol + g)),
        pl.BlockSpec((R, ds), lambda b, g, c: (row(b, g, c), bcol + g)),
        pl.BlockSpec((R, ds), lambda b, g, c: (row(b, g, c), ccol + g)),
        pl.BlockSpec((R, gw), lambda b, g, c: (row(b, g, c), g)),
        pl.BlockSpec((R, LANES), lambda b, g, c: (row(b, g, c), dtcol)),
        pl.BlockSpec((CONV_K, gw), lambda b, g, c: (0, g)),
        pl.BlockSpec((CONV_K, ds), lambda b, g, c: (0, kw + g)),
        pl.BlockSpec((CONV_K, ds), lambda b, g, c: (0, kw + SSM_GROUPS + g)),
        pl.BlockSpec((1, gw), lambda b, g, c: (0, g)),
        pl.BlockSpec((1, ds), lambda b, g, c: (0, kw + g)),
        pl.BlockSpec((1, ds), lambda b, g, c: (0, kw + SSM_GROUPS + g)),
        pl.BlockSpec((None, 1, LANES), lambda b, g, c: (g, 0, 0)),
        pl.BlockSpec((None, 1, LANES), lambda b, g, c: (g, 0, 0)),
        pl.BlockSpec((1, gw), lambda b, g, c: (0, g)),
        pl.BlockSpec((1, gw), lambda b, g, c: (0, g)),
    ]
    out_specs = [
        pl.BlockSpec((R, gw), lambda b, g, c: (row(b, g, c), g)),
        pl.BlockSpec((None, gw, ds), lambda b, g, c: (b, g, 0)),
        pl.BlockSpec((None, CONV_K - 1, gw), lambda b, g, c: (b, 0, g)),
        pl.BlockSpec((None, CONV_K - 1, ds), lambda b, g, c: (b, 0, g)),
        pl.BlockSpec((None, CONV_K - 1, ds), lambda b, g, c: (b, 0, g)),
    ]
    out_shape = [
        jax.ShapeDtypeStruct((m, D_INNER), F32),
        jax.ShapeDtypeStruct((batch, SSM_HEADS * SSM_HEAD_DIM, ds), F32),
        jax.ShapeDtypeStruct((batch, CONV_K - 1, D_INNER), F32),
        jax.ShapeDtypeStruct((batch, CONV_K - 1, SSM_GROUPS * ds), F32),
        jax.ShapeDtypeStruct((batch, CONV_K - 1, SSM_GROUPS * ds), F32),
    ]
    y, st, cvx, cvb, cvc = pl.pallas_call(
        _ssd_kernel,
        grid=(batch, SSM_GROUPS, nc),
        in_specs=in_specs,
        out_specs=out_specs,
        out_shape=out_shape,
        scratch_shapes=[pltpu.VMEM((R + 8, gw), F32), pltpu.VMEM((R + 8, ds), F32),
                        pltpu.VMEM((R + 8, ds), F32), pltpu.VMEM((gw, ds), F32)],
        compiler_params=_cparams(("arbitrary", "arbitrary", "arbitrary")),
        name="ssd_prompt",
    )(proj, proj, proj, proj, proj,
      prm["conv_w"], prm["conv_w"], prm["conv_w"], prm["conv_b"], prm["conv_b"], prm["conv_b"],
      prm["dt_bias_g"], prm["a_log_g"], prm["d_exp"], prm["norm_w"])
    conv = jnp.concatenate([cvx, cvb, cvc], axis=-1)
    return y, st.reshape(batch, SSM_HEADS, SSM_HEAD_DIM, ds), conv


def _ssd_sample_kernel(x_ref, b_ref, c_ref, z_ref, dt_ref, cvx_ref, cvb_ref, cvc_ref, h_ref,
                       cwx_ref, cwb_ref, cwc_ref, cbx_ref, cbb_ref, cbc_ref,
                       dtb_ref, alog_ref, dsk_ref, nw_ref, e4_ref,
                       y_ref, ho_ref, nvx_ref, nvb_ref, nvc_ref, y_s):
    g = pl.program_id(0)
    bd = x_ref.shape[0]

    def conv(raw_ref, st_ref, new_ref, cw_ref, cb_ref):
        raw = raw_ref[...]
        acc = cb_ref[...] + cw_ref[CONV_K - 1:CONV_K, :] * raw
        for k in range(CONV_K - 1):
            acc = acc + cw_ref[k:k + 1, :] * st_ref[k]
        for k in range(CONV_K - 2):
            new_ref[k] = st_ref[k + 1]
        new_ref[CONV_K - 2] = raw
        return _silu(acc)

    xs = conv(x_ref, cvx_ref, nvx_ref, cwx_ref, cbx_ref)
    bm = conv(b_ref, cvb_ref, nvb_ref, cwb_ref, cbb_ref)
    cm = conv(c_ref, cvc_ref, nvc_ref, cwc_ref, cbc_ref)

    shift = (LANES - HEADS_PER_GROUP * g) % LANES
    dt = _softplus(pltpu.roll(dt_ref[...], shift, 1) + dtb_ref[...])
    da = jnp.exp(dt * (-jnp.exp(alog_ref[...])))
    e4 = e4_ref[...]
    dt_e = _dot_exact(dt, e4)
    da_e = _dot_exact(da, e4)
    xdt = xs * dt_e
    pad = jnp.zeros((LANES - bd, GROUP_WIDTH), F32)
    xdt_t = jnp.concatenate([xdt, pad], axis=0).T
    da_t = jnp.concatenate([da_e, pad], axis=0).T
    cmb = cm.astype(BF16)

    for b in range(bd):
        h_new = da_t[:, b:b + 1] * h_ref[b] + xdt_t[:, b:b + 1] * bm[b:b + 1, :]
        ho_ref[b] = h_new
        yb = _dot_nt(jnp.broadcast_to(cmb[b:b + 1, :], (8, D_STATE)), h_new.astype(BF16))
        y_s[b:b + 1, :] = yb[0:1]

    y = y_s[...] + xs * dsk_ref[...]
    y_ref[...] = _gate_norm(y, z_ref[...], nw_ref[...])


def _ssd_sample(proj, conv_state, h0, prm, e4):
    bd = proj.shape[0]
    gw, ds = GROUP_WIDTH, D_STATE
    xcol = D_INNER // gw
    bcol = (2 * D_INNER) // ds
    ccol = bcol + SSM_GROUPS
    dtcol = DT_COL // LANES
    cst = jnp.transpose(conv_state, (1, 0, 2))
    h0v = h0.reshape(bd, SSM_HEADS * SSM_HEAD_DIM, ds)
    kw = D_INNER // ds
    in_specs = [
        pl.BlockSpec((bd, gw), lambda g: (0, xcol + g)),
        pl.BlockSpec((bd, ds), lambda g: (0, bcol + g)),
        pl.BlockSpec((bd, ds), lambda g: (0, ccol + g)),
        pl.BlockSpec((bd, gw), lambda g: (0, g)),
        pl.BlockSpec((bd, LANES), lambda g: (0, dtcol)),
        pl.BlockSpec((CONV_K - 1, bd, gw), lambda g: (0, 0, g)),
        pl.BlockSpec((CONV_K - 1, bd, ds), lambda g: (0, 0, kw + g)),
        pl.BlockSpec((CONV_K - 1, bd, ds), lambda g: (0, 0, kw + SSM_GROUPS + g)),
        pl.BlockSpec((bd, gw, ds), lambda g: (0, g, 0)),
        pl.BlockSpec((CONV_K, gw), lambda g: (0, g)),
        pl.BlockSpec((CONV_K, ds), lambda g: (0, kw + g)),
        pl.BlockSpec((CONV_K, ds), lambda g: (0, kw + SSM_GROUPS + g)),
        pl.BlockSpec((1, gw), lambda g: (0, g)),
        pl.BlockSpec((1, ds), lambda g: (0, kw + g)),
        pl.BlockSpec((1, ds), lambda g: (0, kw + SSM_GROUPS + g)),
        pl.BlockSpec((None, 1, LANES), lambda g: (g, 0, 0)),
        pl.BlockSpec((None, 1, LANES), lambda g: (g, 0, 0)),
        pl.BlockSpec((1, gw), lambda g: (0, g)),
        pl.BlockSpec((1, gw), lambda g: (0, g)),
        pl.BlockSpec((LANES, gw), lambda g: (0, 0)),
    ]
    out_specs = [
        pl.BlockSpec((bd, gw), lambda g: (0, g)),
        pl.BlockSpec((bd, gw, ds), lambda g: (0, g, 0)),
        pl.BlockSpec((CONV_K - 1, bd, gw), lambda g: (0, 0, g)),
        pl.BlockSpec((CONV_K - 1, bd, ds), lambda g: (0, 0, g)),
        pl.BlockSpec((CONV_K - 1, bd, ds), lambda g: (0, 0, g)),
    ]
    out_shape = [
        jax.ShapeDtypeStruct((bd, D_INNER), F32),
        jax.ShapeDtypeStruct((bd, SSM_HEADS * SSM_HEAD_DIM, ds), F32),
        jax.ShapeDtypeStruct((CONV_K - 1, bd, D_INNER), F32),
        jax.ShapeDtypeStruct((CONV_K - 1, bd, SSM_GROUPS * ds), F32),
        jax.ShapeDtypeStruct((CONV_K - 1, bd, SSM_GROUPS * ds), F32),
    ]
    y, hn, nvx, nvb, nvc = pl.pallas_call(
        _ssd_sample_kernel,
        grid=(SSM_GROUPS,),
        in_specs=in_specs,
        out_specs=out_specs,
        out_shape=out_shape,
        scratch_shapes=[pltpu.VMEM((bd, gw), F32)],
        compiler_params=_cparams(("arbitrary",)),
        name="ssd_sample",
    )(proj, proj, proj, proj, proj, cst, cst, cst, h0v,
      prm["conv_w"], prm["conv_w"], prm["conv_w"], prm["conv_b"], prm["conv_b"], prm["conv_b"],
      prm["dt_bias_g"], prm["a_log_g"], prm["d_exp"], prm["norm_w"], e4)
    conv = jnp.transpose(jnp.concatenate([nvx, nvb, nvc], axis=-1), (1, 0, 2))
    return y, hn.reshape(bd, SSM_HEADS, SSM_HEAD_DIM, ds), conv


def _group_lanes(v):
    v = v.reshape(SSM_GROUPS, 1, HEADS_PER_GROUP).astype(F32)
    return jnp.pad(v, ((0, 0), (0, 0), (0, LANES - HEADS_PER_GROUP)))


def kernel(x_prompt, x_sample, cache_kv_w128, cache_kv_w512, cache_kv_w2048, state_ssm, state_conv,
           attn_w_in, attn_w_out, ssm_w_in, ssm_conv_w, ssm_conv_b, ssm_dt_bias, ssm_a_log, ssm_d,
           ssm_norm_w, ssm_w_out, mlp_w1, mlp_w2, ln_mix_g, ln_mix_b, ln_ffn_g, ln_ffn_b):
    caches = (cache_kv_w128, cache_kv_w512, cache_kv_w2048)
    bp, seq, d = x_prompt.shape
    bd = x_sample.shape[0]
    mp = bp * seq
    xp = x_prompt.reshape(mp, d)
    xs = x_sample.reshape(bd, d)

    tm_p = 1024 if seq % 1024 == 0 else seq
    tm_qkv = 512 if seq % 512 == 0 else seq
    tm_ssm = 256 if seq % 256 == 0 else seq
    rope_p = _rope_tables(jnp.arange(seq, dtype=jnp.int32), SCALE * LOG2E)
    rope_s = _rope_tables(jnp.full((bd,), PAST_LEN, dtype=jnp.int32), SCALE)
    e4 = ((jnp.arange(GROUP_WIDTH) // SSM_HEAD_DIM)[None, :] == jnp.arange(LANES)[:, None]).astype(BF16)

    caches_t = [jnp.transpose(c, (0, 1, 3, 4, 5, 2)) for c in caches]
    updated = None
    kv_p = [[] for _ in ATTN_GROUPS]
    ssm_p, conv_p, ssm_s, conv_s = [], [], [], []

    for i in range(DEPTH):
        j = i // 2
        if i % 2 == 0:
            w_in = attn_w_in[j].astype(BF16)
            w_out = attn_w_out[j].astype(BF16)
            qkv_p = _project(xp, w_in, tm_qkv, QKV_WIDTH, rope=rope_p, table_blocks=seq // tm_qkv)
            qkv_s = _project(xs, w_in, bd, QKV_WIDTH, rope=rope_s)
            hp, *tails = _attention_prompt(qkv_p, bp, seq)
            hs, updated = _cache_attention(qkv_s, caches_t, updated, j)
            for gi in range(N_GROUPS):
                kv_p[gi].append(tails[gi])
        else:
            w_in = jnp.pad(ssm_w_in[j], ((0, 0), (0, SSM_IN_PAD - ssm_w_in.shape[2]))).astype(BF16)
            w_out = ssm_w_out[j].astype(BF16)
            prm = {
                "conv_w": ssm_conv_w[j],
                "conv_b": ssm_conv_b[j].reshape(1, CONV_DIM),
                "dt_bias_g": _group_lanes(ssm_dt_bias[j]),
                "a_log_g": _group_lanes(ssm_a_log[j]),
                "d_exp": jnp.repeat(ssm_d[j].astype(F32), SSM_HEAD_DIM).reshape(1, D_INNER),
                "norm_w": ssm_norm_w[j].reshape(1, D_INNER),
            }
            proj_p = _project(xp, w_in, tm_ssm, SSM_IN_PAD)
            proj_s = _project(xs, w_in, bd, SSM_IN_PAD)
            hp, st_p, cv_p = _ssd_prompt(proj_p, prm, bp, seq)
            hs, st_s, cv_s = _ssd_sample(proj_s, state_conv[j], state_ssm[j], prm, e4)
            ssm_p.append(st_p)
            conv_p.append(cv_p)
            ssm_s.append(st_s)
            conv_s.append(cv_s)
        xp = _out_ln(hp, w_out, xp, ln_mix_g[i], ln_mix_b[i], 512)
        xs = _out_ln(hs, w_out, xs, ln_mix_g[i], ln_mix_b[i], bd)
        w1 = mlp_w1[i].astype(BF16)
        w2 = mlp_w2[i].astype(BF16)
        xp = _mlp(xp, w1, w2, ln_ffn_g[i], ln_ffn_b[i], tm_p)
        xs = _mlp(xs, w1, w2, ln_ffn_g[i], ln_ffn_b[i], bd)

    back = (0, 1, 5, 2, 3, 4)
    kv_p_out = [jnp.transpose(jnp.stack(t), back) for t in kv_p]
    kv_s_out = [jnp.transpose(u, back) for u in updated]
    return (xp.reshape(bp, seq, d), xs.reshape(bd, 1, d),
            kv_p_out[0], kv_p_out[1], kv_p_out[2], jnp.stack(ssm_p), jnp.stack(conv_p),
            kv_s_out[0], kv_s_out[1], kv_s_out[2], jnp.stack(ssm_s), jnp.stack(conv_s))
```

```python
import functools

import jax
import jax.numpy as jnp
from jax import lax
from jax.experimental import pallas as pl
from jax.experimental.pallas import tpu as pltpu

F32 = jnp.float32
BF16 = jnp.bfloat16

D_MODEL = 1024
DEPTH = 4
PAST_LEN = 8192
ATTN_GROUPS = ((128, 1), (512, 4), (2048, 16))
ATTN_UNROLL = ((8, 1), (8, 1), (2, 2))
N_GROUPS = len(ATTN_GROUPS)
NSTEP = 128
ATTN_HEADS = 16
HEAD_DIM = 64
ATTN_WIDTH = ATTN_HEADS * HEAD_DIM
QKV_WIDTH = 3 * ATTN_WIDTH
CACHE_HEADS = (4, 2)
ROPE_THETA = 10000.0
SCALE = HEAD_DIM ** -0.5
LOG2E = 1.4426950408889634

D_INNER = 2048
SSM_HEAD_DIM = 64
SSM_HEADS = 32
SSM_GROUPS = 8
HEADS_PER_GROUP = 4
GROUP_WIDTH = HEADS_PER_GROUP * SSM_HEAD_DIM
D_STATE = 128
CONV_K = 4
CONV_DIM = D_INNER + 2 * SSM_GROUPS * D_STATE
DT_COL = D_INNER + CONV_DIM
SSM_IN_PAD = DT_COL + 128
SSD_CHUNK = 128
SSD_SUBCHUNKS = 8

D_FF = 4096
ALPHA = (2 * DEPTH) ** 0.25
LN_EPS = 1e-5
RMS_EPS = 1e-5

LANES = 128
NEG = -1e30
VMEM_LIMIT = 56 * 1024 * 1024


def _cparams(sem):
    return pltpu.CompilerParams(dimension_semantics=sem, vmem_limit_bytes=VMEM_LIMIT)


def _dot(a, b):
    return jnp.dot(a, b, preferred_element_type=F32)


def _dot_nt(a, b):
    return lax.dot_general(a, b, (((1,), (1,)), ((), ())), preferred_element_type=F32)


def _split3(x):
    hi = x.astype(BF16)
    r1 = x - hi.astype(F32)
    mid = r1.astype(BF16)
    lo = (r1 - mid.astype(F32)).astype(BF16)
    return hi, mid, lo


def _dot_exact(x, e):
    hi, mid, lo = _split3(x)
    return _dot(hi, e) + _dot(mid, e) + _dot(lo, e)


def _dot_exact_rhs(e, x):
    hi, mid, lo = _split3(x)
    return _dot(e, hi) + _dot(e, mid) + _dot(e, lo)


def _silu(x):
    return x / (1.0 + jnp.exp(-x))


def _softplus(x):
    return jnp.maximum(x, 0.0) + jnp.log(1.0 + jnp.exp(-jnp.abs(x)))


def _layer_norm(y, g, b):
    mu = jnp.mean(y, axis=-1, keepdims=True)
    yc = y - mu
    var = jnp.mean(yc * yc, axis=-1, keepdims=True)
    return yc * lax.rsqrt(var + LN_EPS) * g + b


def _proj_kernel(x_ref, w_ref, o_ref):
    xb = x_ref[...].astype(BF16)
    chunk = 4 * LANES
    n = o_ref.shape[1]
    for lo in range(0, n, chunk):
        hi = min(lo + chunk, n)
        o_ref[:, lo:hi] = _dot(xb, w_ref[:, lo:hi])


def _proj_rope_kernel(x_ref, w_ref, cos_ref, sa_ref, sb_ref, o_ref):
    xb = x_ref[...].astype(BF16)
    chunk = 2 * LANES
    for kind in range(3):
        cos, sa, sb = cos_ref[kind], sa_ref[kind], sb_ref[kind]
        for c in range(ATTN_WIDTH // chunk):
            base = kind * ATTN_WIDTH + c * chunk
            acc = _dot(xb, w_ref[:, base:base + chunk])
            for h in range(chunk // LANES):
                seg = acc[:, h * LANES:(h + 1) * LANES]
                lo = base + h * LANES
                o_ref[:, lo:lo + LANES] = (
                    seg * cos + pltpu.roll(seg, 96, 1) * sa + pltpu.roll(seg, 32, 1) * sb)


def _project(x, w, tm, tn, rope=None, table_blocks=1):
    m, k = x.shape
    n = w.shape[1]
    grid = (n // tn, m // tm)
    in_specs = [pl.BlockSpec((tm, k), lambda j, i: (i, 0)),
                pl.BlockSpec((k, tn), lambda j, i: (0, j))]
    args = [x, w]
    if rope is None:
        body = _proj_kernel
    else:
        body = _proj_rope_kernel
        tspec = pl.BlockSpec((3, tm, LANES), lambda j, i: (0, i % table_blocks, 0))
        in_specs += [tspec, tspec, tspec]
        args += list(rope)
    return pl.pallas_call(
        body,
        grid=grid,
        in_specs=in_specs,
        out_specs=pl.BlockSpec((tm, tn), lambda j, i: (i, j)),
        out_shape=jax.ShapeDtypeStruct((m, n), F32),
        compiler_params=_cparams(("arbitrary", "arbitrary")),
        name="proj_rope" if rope is not None else "proj",
    )(*args)


def _rope_tables(pos, q_scale):
    half = HEAD_DIM // 2
    inv = jnp.power(ROPE_THETA, -jnp.arange(half, dtype=F32) * (2.0 / HEAD_DIM))
    ang = pos.astype(F32)[:, None] * inv[None, :]
    cos = jnp.tile(jnp.cos(ang), (1, LANES // half))
    sin = jnp.tile(jnp.sin(ang), (1, LANES // half))
    first = (jnp.arange(LANES) % HEAD_DIM) < half
    sa = jnp.where(first[None, :], -sin, 0.0)
    sb = jnp.where(first[None, :], 0.0, sin)
    one, zero = jnp.ones_like(cos), jnp.zeros_like(cos)
    return (jnp.stack([cos * q_scale, cos, one]), jnp.stack([sa * q_scale, sa, zero]),
            jnp.stack([sb * q_scale, sb, zero]))


def _attn_kernel(q_ref, k_ref, v_ref, o_ref, kv0_ref, kv1_ref, kv2_ref, m_s, l_s, acc_s, *, seq):
    g = pl.program_id(2)
    lane_a = lax.broadcasted_iota(jnp.int32, (NSTEP, LANES), 1) < HEAD_DIM
    kj = lax.broadcasted_iota(jnp.int32, (NSTEP, 2 * NSTEP), 1)
    dmat = kj - lax.broadcasted_iota(jnp.int32, (NSTEP, 2 * NSTEP), 0)
    bias_full = jnp.where(dmat >= 0, jnp.where(dmat <= NSTEP, 0.0, NEG), NEG).astype(F32)
    bias_own = jnp.where(kj >= NSTEP, bias_full, NEG)
    lane_a2 = lax.broadcasted_iota(jnp.int32, (2 * NSTEP, LANES), 1) < HEAD_DIM
    ones = jnp.ones((2 * NSTEP, LANES), BF16)

    def run_group(dil, first, unroll, res_per_iter):
        nb = seq // dil // NSTEP
        niter = nb // unroll
        span = unroll * NSTEP

        def rows(start, size):
            if dil == 1:
                return pl.ds(pl.multiple_of(start, NSTEP), size)
            return pl.ds(start, size, stride=dil)

        def step(t, carry):
            n0 = (t % niter) * unroll
            bias0 = jnp.where(n0 > 0, bias_full, bias_own)
            zero = jnp.zeros((NSTEP, LANES), BF16)
            chains, q_rows_of = [], []
            for rr in range(res_per_iter):
                r = (t // niter) * res_per_iter + rr
                base = r + dil * NSTEP * n0
                prev = r + dil * NSTEP * jnp.maximum(n0 - 1, 0)
                q_all = q_ref[rows(base, span), :].astype(BF16)
                k_all = k_ref[rows(base, span), :].astype(BF16)
                v_all = v_ref[rows(base, span), :].astype(BF16)
                k_prev = k_ref[rows(prev, NSTEP), :].astype(BF16)
                v_prev = v_ref[rows(prev, NSTEP), :].astype(BF16)
                for u in range(unroll):
                    qb = q_all[u * NSTEP:(u + 1) * NSTEP]
                    if u == 0:
                        kw = jnp.concatenate([k_prev, k_all[0:NSTEP]], axis=0)
                        vw = jnp.concatenate([v_prev, v_all[0:NSTEP]], axis=0)
                        bias = bias0
                    else:
                        kw = k_all[(u - 1) * NSTEP:(u + 1) * NSTEP]
                        vw = v_all[(u - 1) * NSTEP:(u + 1) * NSTEP]
                        bias = bias_full
                    chains.append((jnp.where(lane_a, qb, zero), kw, jnp.where(lane_a2, vw, ones), bias))
                    chains.append((jnp.where(lane_a, zero, qb), kw, jnp.where(lane_a2, ones, vw), bias))
                    q_rows_of.append(rows(base + dil * NSTEP * u, NSTEP))
            scores = [_dot_nt(qh, kw) + bias for qh, kw, _, bias in chains]
            maxes = [jnp.max(s, axis=1, keepdims=True) for s in scores]
            probs = [jnp.exp2(s - mx).astype(BF16) for s, mx in zip(scores, maxes)]
            pvs = [_dot(p, ch[2]) for p, ch in zip(probs, chains)]
            for u, q_rows in enumerate(q_rows_of):
                o_a, o_b = pvs[2 * u], pvs[2 * u + 1]
                m_blk = jnp.where(lane_a, maxes[2 * u], maxes[2 * u + 1])
                o_blk = jnp.where(lane_a, o_a, o_b)
                l_blk = pltpu.roll(jnp.where(lane_a, o_b, o_a), HEAD_DIM, 1)
                if first:
                    m_s[q_rows, :] = m_blk
                    l_s[q_rows, :] = l_blk
                    acc_s[q_rows, :] = o_blk
                else:
                    m_old = m_s[q_rows, :]
                    m_new = jnp.maximum(m_old, m_blk)
                    a_old = jnp.exp2(m_old - m_new)
                    a_blk = jnp.exp2(m_blk - m_new)
                    m_s[q_rows, :] = m_new
                    l_s[q_rows, :] = a_old * l_s[q_rows, :] + a_blk * l_blk
                    acc_s[q_rows, :] = a_old * acc_s[q_rows, :] + a_blk * o_blk
            return carry

        lax.fori_loop(0, (dil // res_per_iter) * niter, step, 0)

    def write_tail(kv_ref, keep):
        for j in range(keep // LANES):
            r0 = seq - keep + j * LANES
            kt = k_ref[r0:r0 + LANES, :].T
            vt = v_ref[r0:r0 + LANES, :].T
            cols = slice(j * LANES, (j + 1) * LANES)
            kv_ref[0, 0, :, cols] = kt[0:HEAD_DIM]
            kv_ref[0, 1, :, cols] = kt[HEAD_DIM:2 * HEAD_DIM]
            kv_ref[1, 0, :, cols] = vt[0:HEAD_DIM]
            kv_ref[1, 1, :, cols] = vt[HEAD_DIM:2 * HEAD_DIM]

    for gi, ((win, dil), kv_ref) in enumerate(zip(ATTN_GROUPS, (kv0_ref, kv1_ref, kv2_ref))):
        @pl.when(g == N_GROUPS - 1 - gi)
        def _(dil=dil, gi=gi, win=win, kv_ref=kv_ref):
            run_group(dil, gi == N_GROUPS - 1, *ATTN_UNROLL[gi])
            write_tail(kv_ref, min(win, seq))

    @pl.when(g == N_GROUPS - 1)
    def _():
        o_ref[...] = (acc_s[...] / l_s[...]).astype(o_ref.dtype)


def _attention_prompt(qkv, batch, seq):
    m = qkv.shape[0]
    cpg = QKV_WIDTH // LANES
    cpk = ATTN_WIDTH // LANES
    last = N_GROUPS - 1
    blk = (seq, LANES)
    keeps = [min(win, seq) for win, _ in ATTN_GROUPS]
    return pl.pallas_call(
        functools.partial(_attn_kernel, seq=seq),
        grid=(batch, cpk, N_GROUPS),
        in_specs=[pl.BlockSpec(blk, lambda b, hp, g: (b, (last - g) * cpg + hp)),
                  pl.BlockSpec(blk, lambda b, hp, g: (b, (last - g) * cpg + cpk + hp)),
                  pl.BlockSpec(blk, lambda b, hp, g: (b, (last - g) * cpg + 2 * cpk + hp))],
        out_specs=[pl.BlockSpec(blk, lambda b, hp, g: (b, hp))]
                  + [pl.BlockSpec((None, 2, 2, HEAD_DIM, keep), lambda b, hp, g: (b, 0, hp, 0, 0))
                     for keep in keeps],
        out_shape=[jax.ShapeDtypeStruct((m, ATTN_WIDTH), BF16)]
                  + [jax.ShapeDtypeStruct((batch, 2, ATTN_HEADS, HEAD_DIM, keep), F32) for keep in keeps],
        scratch_shapes=[pltpu.VMEM(blk, F32)] * 3,
        compiler_params=_cparams(("arbitrary", "arbitrary", "arbitrary")),
        name="attn_prompt",
    )(qkv, qkv, qkv)


def _eye64():
    return (lax.broadcasted_iota(jnp.int32, (HEAD_DIM, HEAD_DIM), 0)
            == lax.broadcasted_iota(jnp.int32, (HEAD_DIM, HEAD_DIM), 1))


def _row_to_col(row, eye):
    return jnp.sum(jnp.where(eye, jnp.broadcast_to(row, (HEAD_DIM, HEAD_DIM)), 0.0), axis=1, keepdims=True)


def _col_to_row(col, eye):
    return jnp.sum(jnp.where(eye, jnp.broadcast_to(col, (HEAD_DIM, HEAD_DIM)), 0.0), axis=0, keepdims=True)


def _sample_group(qkv_ref, gi, hh, kt, vt, dil, eye):
    q = _row_to_col(qkv_ref[3 * gi, hh], eye)
    kn = _row_to_col(qkv_ref[3 * gi + 1, hh], eye)
    vn = _row_to_col(qkv_ref[3 * gi + 2, hh], eye)
    s = jnp.sum(q * kt, axis=0, keepdims=True)
    if dil > 1:
        pos = lax.broadcasted_iota(jnp.int32, s.shape, 1)
        s = jnp.where((pos & (dil - 1)) == 0, s, NEG)
    sn = jnp.sum(q * kn, axis=0, keepdims=True)
    mx = jnp.maximum(jnp.max(s, axis=1, keepdims=True), sn)
    p = jnp.exp(s - mx)
    pn = jnp.exp(sn - mx)
    den = jnp.sum(p, axis=1, keepdims=True) + pn
    o = (jnp.sum(p * vt, axis=1, keepdims=True) + pn * vn) / den
    return o, mx + jnp.log(den), kn, vn


def _merge_groups(outs, lses, eye):
    top = jnp.maximum(jnp.maximum(lses[0], lses[1]), lses[2])
    ws = [jnp.exp(l - top) for l in lses]
    acc = ws[0] * outs[0] + ws[1] * outs[1] + ws[2] * outs[2]
    return _col_to_row(acc / (ws[0] + ws[1] + ws[2]), eye)


def _shift_left(x):
    return pltpu.roll(x, x.shape[1] - 1, 1)


def _shift_in(x, new_col):
    lane = lax.broadcasted_iota(jnp.int32, x.shape, 1)
    return jnp.where(lane == x.shape[1] - 1, new_col, _shift_left(x))


def _cache_attn_first_kernel(qkv_ref, c0_ref, c1_ref, c2_ref, o_ref, n0_ref, n1_ref, n2_ref):
    eye = _eye64()
    for hh in range(o_ref.shape[0]):
        outs, lses = [], []
        for gi, (c_ref, n_ref) in enumerate(zip((c0_ref, c1_ref, c2_ref), (n0_ref, n1_ref, n2_ref))):
            kt, vt = c_ref[0, 0, hh], c_ref[0, 1, hh]
            o, lse, kn, vn = _sample_group(qkv_ref, gi, hh, kt, vt, ATTN_GROUPS[gi][1], eye)
            outs.append(o)
            lses.append(lse)
            n_ref[0, 0, hh] = _shift_in(kt, kn)
            n_ref[0, 1, hh] = _shift_in(vt, vn)
            for layer in range(1, c_ref.shape[0]):
                n_ref[layer, 0, hh] = _shift_left(c_ref[layer, 0, hh])
                n_ref[layer, 1, hh] = _shift_left(c_ref[layer, 1, hh])
        o_ref[hh] = _merge_groups(outs, lses, eye)


def _cache_attn_later_kernel(qkv_ref, c0_ref, c1_ref, c2_ref, a0_ref, a1_ref, a2_ref,
                             o_ref, n0_ref, n1_ref, n2_ref):
    del a0_ref, a1_ref, a2_ref
    eye = _eye64()
    for hh in range(o_ref.shape[0]):
        outs, lses = [], []
        for gi, (c_ref, n_ref) in enumerate(zip((c0_ref, c1_ref, c2_ref), (n0_ref, n1_ref, n2_ref))):
            kt, vt = c_ref[0, hh], c_ref[1, hh]
            o, lse, kn, vn = _sample_group(qkv_ref, gi, hh, kt, vt, ATTN_GROUPS[gi][1], eye)
            outs.append(o)
            lses.append(lse)
            win = kt.shape[1]
            n_ref[0, hh] = _shift_in(kt[:, win - LANES:], kn)
            n_ref[1, hh] = _shift_in(vt[:, win - LANES:], vn)
        o_ref[hh] = _merge_groups(outs, lses, eye)


def _cache_attention(qkv, caches_t, updated, layer):
    bd = qkv.shape[0]
    q5 = qkv.reshape(bd, 3 * N_GROUPS, ATTN_HEADS, 1, HEAD_DIM)
    hps = CACHE_HEADS[0] if layer == 0 else CACHE_HEADS[1]
    grid = (bd, ATTN_HEADS // hps)
    q_spec = pl.BlockSpec((None, 3 * N_GROUPS, hps, 1, HEAD_DIM), lambda b, h: (b, 0, h, 0, 0))
    o_spec = pl.BlockSpec((None, hps, 1, HEAD_DIM), lambda b, h: (b, h, 0, 0))
    o_shape = jax.ShapeDtypeStruct((bd, ATTN_HEADS, 1, HEAD_DIM), F32)
    n_shapes = [jax.ShapeDtypeStruct(c.shape, F32) for c in caches_t]
    if layer == 0:
        c_specs = [pl.BlockSpec((c.shape[0], None, 2, hps, HEAD_DIM, c.shape[5]),
                                lambda b, h: (0, b, 0, h, 0, 0)) for c in caches_t]
        res = pl.pallas_call(
            _cache_attn_first_kernel,
            grid=grid,
            in_specs=[q_spec] + c_specs,
            out_specs=[o_spec] + c_specs,
            out_shape=[o_shape] + n_shapes,
            compiler_params=_cparams(("arbitrary", "arbitrary")),
            name="cache_attn_first",
        )(q5, *caches_t)
    else:
        c_specs = [pl.BlockSpec((None, None, 2, hps, HEAD_DIM, c.shape[5]),
                                lambda b, h, layer=layer: (layer, b, 0, h, 0, 0)) for c in caches_t]
        n_specs = [pl.BlockSpec((None, None, 2, hps, HEAD_DIM, LANES),
                                lambda b, h, layer=layer, last=c.shape[5] // LANES - 1: (layer, b, 0, h, 0, last))
                   for c in caches_t]
        any_spec = pl.BlockSpec(memory_space=pl.ANY)
        res = pl.pallas_call(
            _cache_attn_later_kernel,
            grid=grid,
            in_specs=[q_spec] + c_specs + [any_spec] * N_GROUPS,
            out_specs=[o_spec] + n_specs,
            out_shape=[o_shape] + n_shapes,
            input_output_aliases={1 + N_GROUPS + gi: 1 + gi for gi in range(N_GROUPS)},
            compiler_params=_cparams(("arbitrary", "arbitrary")),
            name="cache_attn_later",
        )(q5, *caches_t, *updated)
    return res[0].reshape(bd, ATTN_WIDTH), list(res[1:])


def _out_ln_kernel(a_ref, w_ref, x_ref, g_ref, b_ref, o_ref):
    tm = o_ref.shape[0]
    sub = min(tm, LANES)
    for r in range(0, tm, sub):
        y = _dot(a_ref[r:r + sub, :].astype(BF16), w_ref[...]) + ALPHA * x_ref[r:r + sub, :]
        o_ref[r:r + sub, :] = _layer_norm(y, g_ref[...], b_ref[...])


def _out_ln(a, w, x, g, b, tm):
    m, k = a.shape
    d = w.shape[1]
    return pl.pallas_call(
        _out_ln_kernel,
        grid=(m // tm,),
        in_specs=[pl.BlockSpec((tm, k), lambda i: (i, 0)),
                  pl.BlockSpec((k, d), lambda i: (0, 0)),
                  pl.BlockSpec((tm, d), lambda i: (i, 0)),
                  pl.BlockSpec((1, d), lambda i: (0, 0)),
                  pl.BlockSpec((1, d), lambda i: (0, 0))],
        out_specs=pl.BlockSpec((tm, d), lambda i: (i, 0)),
        out_shape=jax.ShapeDtypeStruct((m, d), F32),
        compiler_params=_cparams(("arbitrary",)),
        name="out_ln",
    )(a, w, x, g.reshape(1, d), b.reshape(1, d))


def _mlp_kernel(x_ref, w1_ref, w2_ref, g_ref, b_ref, o_ref, xb_s, acc_s):
    f = pl.program_id(1)

    @pl.when(f == 0)
    def _():
        x = x_ref[...]
        xb_s[...] = x.astype(BF16)
        acc_s[...] = ALPHA * x

    h = jnp.maximum(_dot(xb_s[...], w1_ref[...]), 0.0)
    acc_s[...] += _dot((h * h).astype(BF16), w2_ref[...])

    @pl.when(f == pl.num_programs(1) - 1)
    def _():
        o_ref[...] = _layer_norm(acc_s[...], g_ref[...], b_ref[...])


def _mlp(x, w1, w2, g, b, tm, tf=1024):
    m, d = x.shape
    ff = w1.shape[1]
    return pl.pallas_call(
        _mlp_kernel,
        grid=(m // tm, ff // tf),
        in_specs=[pl.BlockSpec((tm, d), lambda i, f: (i, 0)),
                  pl.BlockSpec((d, tf), lambda i, f: (0, f)),
                  pl.BlockSpec((tf, d), lambda i, f: (f, 0)),
                  pl.BlockSpec((1, d), lambda i, f: (0, 0)),
                  pl.BlockSpec((1, d), lambda i, f: (0, 0))],
        out_specs=pl.BlockSpec((tm, d), lambda i, f: (i, 0)),
        out_shape=jax.ShapeDtypeStruct((m, d), F32),
        scratch_shapes=[pltpu.VMEM((tm, d), BF16), pltpu.VMEM((tm, d), F32)],
        compiler_params=_cparams(("arbitrary", "arbitrary")),
        name="mlp",
    )(x, w1, w2, g.reshape(1, d), b.reshape(1, d))


def _gate_norm(y, z, nw):
    y = y * _silu(z)
    ms = jnp.mean(y * y, axis=-1, keepdims=True)
    return y * lax.rsqrt(ms + RMS_EPS) * nw


def _ssd_kernel(x_ref, b_ref, c_ref, z_ref, dt_ref,
                cwx_ref, cwb_ref, cwc_ref, cbx_ref, cbb_ref, cbc_ref,
                dtb_ref, alog_ref, dsk_ref, nw_ref,
                y_ref, st_ref, cvx_ref, cvb_ref, cvc_ref,
                ex_s, eb_s, ec_s, h_s):
    g = pl.program_id(1)
    c = pl.program_id(2)
    L = SSD_CHUNK
    R = x_ref.shape[0]

    @pl.when(c == 0)
    def _():
        ex_s[0:8, :] = jnp.zeros((8, GROUP_WIDTH), F32)
        eb_s[0:8, :] = jnp.zeros((8, D_STATE), F32)
        ec_s[0:8, :] = jnp.zeros((8, D_STATE), F32)
        h_s[...] = jnp.zeros_like(h_s)

    def conv(raw_ref, e_s, cw_ref, cb_ref):
        e_s[8:8 + R, :] = raw_ref[...]
        acc = cb_ref[...] + cw_ref[0:1, :] * e_s[5:5 + R, :]
        for k in range(1, CONV_K):
            acc = acc + cw_ref[k:k + 1, :] * e_s[5 + k:5 + k + R, :]
        return _silu(acc)

    xs = conv(x_ref, ex_s, cwx_ref, cbx_ref)
    bm = conv(b_ref, eb_s, cwb_ref, cbb_ref)
    cm = conv(c_ref, ec_s, cwc_ref, cbc_ref)

    @pl.when(c == pl.num_programs(2) - 1)
    def _():
        cvx_ref[...] = ex_s[5 + R:8 + R, :]
        cvb_ref[...] = eb_s[5 + R:8 + R, :]
        cvc_ref[...] = ec_s[5 + R:8 + R, :]

    ex_s[0:8, :] = ex_s[R:R + 8, :]
    eb_s[0:8, :] = eb_s[R:R + 8, :]
    ec_s[0:8, :] = ec_s[R:R + 8, :]

    shift = (LANES - HEADS_PER_GROUP * g) % LANES
    dt_all = _softplus(pltpu.roll(dt_ref[...], shift, 1) + dtb_ref[...])
    dta_all = dt_all * (-jnp.exp(alog_ref[...]))
    row = lax.broadcasted_iota(jnp.int32, (L, L), 0)
    col = lax.broadcasted_iota(jnp.int32, (L, L), 1)
    causal = row >= col
    tril = jnp.where(causal, 1.0, 0.0).astype(BF16)
    lane_a = col < SSM_HEAD_DIM
    row_a = row < SSM_HEAD_DIM
    npair = HEADS_PER_GROUP // 2
    nsub = R // L
    units = [(sc, pair) for sc in range(nsub) for pair in range(npair)]

    acums = [_dot_exact_rhs(tril, dta_all[sc * L:(sc + 1) * L]) * LOG2E for sc in range(nsub)]
    acum_ts = [a.T for a in acums]
    bmbs = [bm[sc * L:(sc + 1) * L].astype(BF16) for sc in range(nsub)]
    cmbs = [cm[sc * L:(sc + 1) * L].astype(BF16) for sc in range(nsub)]
    cbs = [_dot_nt(c, b) for c, b in zip(cmbs, bmbs)]

    def per_head(a, pair):
        return jnp.where(lane_a, a[:, 2 * pair:2 * pair + 1], a[:, 2 * pair + 1:2 * pair + 2])

    xps = {(sc, p): xs[sc * L:(sc + 1) * L, p * LANES:(p + 1) * LANES] for sc, p in units}
    xdts = {(sc, p): xps[sc, p] * per_head(dt_all[sc * L:(sc + 1) * L], p) for sc, p in units}
    ac_ps = {(sc, p): per_head(acums[sc], p) for sc, p in units}
    masked = {}
    for sc, p in units:
        for hd in (2 * p, 2 * p + 1):
            diff = acums[sc][:, hd:hd + 1] - acum_ts[sc][hd:hd + 1, :]
            masked[sc, hd] = (cbs[sc] * jnp.exp2(jnp.where(causal, diff, NEG))).astype(BF16)
    y_diags, upds = {}, {}
    for sc, p in units:
        xdtb = xdts[sc, p].astype(BF16)
        y_diags[sc, p] = jnp.where(lane_a, _dot(masked[sc, 2 * p], xdtb), _dot(masked[sc, 2 * p + 1], xdtb))
        last_p = per_head(acums[sc][L - 1:L, :], p)
        xw = xdts[sc, p] * jnp.exp2(last_p - ac_ps[sc, p])
        upds[sc, p] = _dot(xw.T.astype(BF16), bmbs[sc])

    for p in range(npair):
        ps = slice(p * LANES, (p + 1) * LANES)
        h = h_s[ps, :]
        for sc in range(nsub):
            rs = slice(sc * L, (sc + 1) * L)
            y_off = _dot_nt(cmbs[sc], h.astype(BF16)) * jnp.exp2(ac_ps[sc, p])
            y_ref[rs, ps] = y_diags[sc, p] + y_off + xps[sc, p] * dsk_ref[:, ps]
            last = acums[sc][L - 1:L, :]
            last_r = jnp.where(row_a, last[:, 2 * p:2 * p + 1], last[:, 2 * p + 1:2 * p + 2])
            h = jnp.exp2(last_r) * h + upds[sc, p]
        h_s[ps, :] = h
    y_ref[...] = _gate_norm(y_ref[...], z_ref[...], nw_ref[...])

    @pl.when(c == pl.num_programs(2) - 1)
    def _():
        st_ref[...] = h_s[...]


def _ssd_prompt(proj, prm, batch, seq):
    m = proj.shape[0]
    R = SSD_CHUNK * SSD_SUBCHUNKS if seq % (SSD_CHUNK * SSD_SUBCHUNKS) == 0 else SSD_CHUNK
    nc = seq // R
    gw, ds = GROUP_WIDTH, D_STATE
    xcol = D_INNER // gw
    bcol = (2 * D_INNER) // ds
    ccol = bcol + SSM_GROUPS
    dtcol = DT_COL // LANES
    kw = D_INNER // ds
    row = lambda b, g, c: b * nc + c
    in_specs = [
        pl.BlockSpec((R, gw), lambda b, g, c: (row(b, g, c), xcol + g)),
        pl.BlockSpec((R, ds), lambda b, g, c: (row(b, g, c), bcol + g)),
        pl.BlockSpec((R, ds), lambda b, g, c: (row(b, g, c), ccol + g)),
        pl.BlockSpec((R, gw), lambda b, g, c: (row(b, g, c), g)),
        pl.BlockSpec((R, LANES), lambda b, g, c: (row(b, g, c), dtcol)),
        pl.BlockSpec((CONV_K, gw), lambda b, g, c: (0, g)),
        pl.BlockSpec((CONV_K, ds), lambda b, g, c: (0, kw + g)),
        pl.BlockSpec((CONV_K, ds), lambda b, g, c: (0, kw + SSM_GROUPS + g)),
        pl.BlockSpec((1, gw), lambda b, g, c: (0, g)),
        pl.BlockSpec((1, ds), lambda b, g, c: (0, kw + g)),
        pl.BlockSpec((1, ds), lambda b, g, c: (0, kw + SSM_GROUPS + g)),
        pl.BlockSpec((None, 1, LANES), lambda b, g, c: (g, 0, 0)),
        pl.BlockSpec((None, 1, LANES), lambda b, g, c: (g, 0, 0)),
        pl.BlockSpec((1, gw), lambda b, g, c: (0, g)),
        pl.BlockSpec((1, gw), lambda b, g, c: (0, g)),
    ]
    out_specs = [
        pl.BlockSpec((R, gw), lambda b, g, c: (row(b, g, c), g)),
        pl.BlockSpec((None, gw, ds), lambda b, g, c: (b, g, 0)),
        pl.BlockSpec((None, CONV_K - 1, gw), lambda b, g, c: (b, 0, g)),
        pl.BlockSpec((None, CONV_K - 1, ds), lambda b, g, c: (b, 0, g)),
        pl.BlockSpec((None, CONV_K - 1, ds), lambda b, g, c: (b, 0, g)),
    ]
    out_shape = [
        jax.ShapeDtypeStruct((m, D_INNER), F32),
        jax.ShapeDtypeStruct((batch, SSM_HEADS * SSM_HEAD_DIM, ds), F32),
        jax.ShapeDtypeStruct((batch, CONV_K - 1, D_INNER), F32),
        jax.ShapeDtypeStruct((batch, CONV_K - 1, SSM_GROUPS * ds), F32),
        jax.ShapeDtypeStruct((batch, CONV_K - 1, SSM_GROUPS * ds), F32),
    ]
    y, st, cvx, cvb, cvc = pl.pallas_call(
        _ssd_kernel,
        grid=(batch, SSM_GROUPS, nc),
        in_specs=in_specs,
        out_specs=out_specs,
        out_shape=out_shape,
        scratch_shapes=[pltpu.VMEM((R + 8, gw), F32), pltpu.VMEM((R + 8, ds), F32),
                        pltpu.VMEM((R + 8, ds), F32), pltpu.VMEM((gw, ds), F32)],
        compiler_params=_cparams(("arbitrary", "arbitrary", "arbitrary")),
        name="ssd_prompt",
    )(proj, proj, proj, proj, proj,
      prm["conv_w"], prm["conv_w"], prm["conv_w"], prm["conv_b"], prm["conv_b"], prm["conv_b"],
      prm["dt_bias_g"], prm["a_log_g"], prm["d_exp"], prm["norm_w"])
    conv = jnp.concatenate([cvx, cvb, cvc], axis=-1)
    return y, st.reshape(batch, SSM_HEADS, SSM_HEAD_DIM, ds), conv


def _ssd_sample_kernel(x_ref, b_ref, c_ref, z_ref, dt_ref, cvx_ref, cvb_ref, cvc_ref, h_ref,
                       cwx_ref, cwb_ref, cwc_ref, cbx_ref, cbb_ref, cbc_ref,
                       dtb_ref, alog_ref, dsk_ref, nw_ref, e4_ref,
                       y_ref, ho_ref, nvx_ref, nvb_ref, nvc_ref, y_s):
    g = pl.program_id(0)
    bd = x_ref.shape[0]

    def conv(raw_ref, st_ref, new_ref, cw_ref, cb_ref):
        raw = raw_ref[...]
        acc = cb_ref[...] + cw_ref[CONV_K - 1:CONV_K, :] * raw
        for k in range(CONV_K - 1):
            acc = acc + cw_ref[k:k + 1, :] * st_ref[k]
        for k in range(CONV_K - 2):
            new_ref[k] = st_ref[k + 1]
        new_ref[CONV_K - 2] = raw
        return _silu(acc)

    xs = conv(x_ref, cvx_ref, nvx_ref, cwx_ref, cbx_ref)
    bm = conv(b_ref, cvb_ref, nvb_ref, cwb_ref, cbb_ref)
    cm = conv(c_ref, cvc_ref, nvc_ref, cwc_ref, cbc_ref)

    shift = (LANES - HEADS_PER_GROUP * g) % LANES
    dt = _softplus(pltpu.roll(dt_ref[...], shift, 1) + dtb_ref[...])
    da = jnp.exp(dt * (-jnp.exp(alog_ref[...])))
    e4 = e4_ref[...]
    dt_e = _dot_exact(dt, e4)
    da_e = _dot_exact(da, e4)
    xdt = xs * dt_e
    pad = jnp.zeros((LANES - bd, GROUP_WIDTH), F32)
    xdt_t = jnp.concatenate([xdt, pad], axis=0).T
    da_t = jnp.concatenate([da_e, pad], axis=0).T
    cmb = cm.astype(BF16)

    for b in range(bd):
        h_new = da_t[:, b:b + 1] * h_ref[b] + xdt_t[:, b:b + 1] * bm[b:b + 1, :]
        ho_ref[b] = h_new
        yb = _dot_nt(jnp.broadcast_to(cmb[b:b + 1, :], (8, D_STATE)), h_new.astype(BF16))
        y_s[b:b + 1, :] = yb[0:1]

    y = y_s[...] + xs * dsk_ref[...]
    y_ref[...] = _gate_norm(y, z_ref[...], nw_ref[...])


def _ssd_sample(proj, conv_state, h0, prm, e4):
    bd = proj.shape[0]
    gw, ds = GROUP_WIDTH, D_STATE
    xcol = D_INNER // gw
    bcol = (2 * D_INNER) // ds
    ccol = bcol + SSM_GROUPS
    dtcol = DT_COL // LANES
    cst = jnp.transpose(conv_state, (1, 0, 2))
    h0v = h0.reshape(bd, SSM_HEADS * SSM_HEAD_DIM, ds)
    kw = D_INNER // ds
    in_specs = [
        pl.BlockSpec((bd, gw), lambda g: (0, xcol + g)),
        pl.BlockSpec((bd, ds), lambda g: (0, bcol + g)),
        pl.BlockSpec((bd, ds), lambda g: (0, ccol + g)),
        pl.BlockSpec((bd, gw), lambda g: (0, g)),
        pl.BlockSpec((bd, LANES), lambda g: (0, dtcol)),
        pl.BlockSpec((CONV_K - 1, bd, gw), lambda g: (0, 0, g)),
        pl.BlockSpec((CONV_K - 1, bd, ds), lambda g: (0, 0, kw + g)),
        pl.BlockSpec((CONV_K - 1, bd, ds), lambda g: (0, 0, kw + SSM_GROUPS + g)),
        pl.BlockSpec((bd, gw, ds), lambda g: (0, g, 0)),
        pl.BlockSpec((CONV_K, gw), lambda g: (0, g)),
        pl.BlockSpec((CONV_K, ds), lambda g: (0, kw + g)),
        pl.BlockSpec((CONV_K, ds), lambda g: (0, kw + SSM_GROUPS + g)),
        pl.BlockSpec((1, gw), lambda g: (0, g)),
        pl.BlockSpec((1, ds), lambda g: (0, kw + g)),
        pl.BlockSpec((1, ds), lambda g: (0, kw + SSM_GROUPS + g)),
        pl.BlockSpec((None, 1, LANES), lambda g: (g, 0, 0)),
        pl.BlockSpec((None, 1, LANES), lambda g: (g, 0, 0)),
        pl.BlockSpec((1, gw), lambda g: (0, g)),
        pl.BlockSpec((1, gw), lambda g: (0, g)),
        pl.BlockSpec((LANES, gw), lambda g: (0, 0)),
    ]
    out_specs = [
        pl.BlockSpec((bd, gw), lambda g: (0, g)),
        pl.BlockSpec((bd, gw, ds), lambda g: (0, g, 0)),
        pl.BlockSpec((CONV_K - 1, bd, gw), lambda g: (0, 0, g)),
        pl.BlockSpec((CONV_K - 1, bd, ds), lambda g: (0, 0, g)),
        pl.BlockSpec((CONV_K - 1, bd, ds), lambda g: (0, 0, g)),
    ]
    out_shape = [
        jax.ShapeDtypeStruct((bd, D_INNER), F32),
        jax.ShapeDtypeStruct((bd, SSM_HEADS * SSM_HEAD_DIM, ds), F32),
        jax.ShapeDtypeStruct((CONV_K - 1, bd, D_INNER), F32),
        jax.ShapeDtypeStruct((CONV_K - 1, bd, SSM_GROUPS * ds), F32),
        jax.ShapeDtypeStruct((CONV_K - 1, bd, SSM_GROUPS * ds), F32),
    ]
    y, hn, nvx, nvb, nvc = pl.pallas_call(
        _ssd_sample_kernel,
        grid=(SSM_GROUPS,),
        in_specs=in_specs,
        out_specs=out_specs,
        out_shape=out_shape,
        scratch_shapes=[pltpu.VMEM((bd, gw), F32)],
        compiler_params=_cparams(("arbitrary",)),
        name="ssd_sample",
    )(proj, proj, proj, proj, proj, cst, cst, cst, h0v,
      prm["conv_w"], prm["conv_w"], prm["conv_w"], prm["conv_b"], prm["conv_b"], prm["conv_b"],
      prm["dt_bias_g"], prm["a_log_g"], prm["d_exp"], prm["norm_w"], e4)
    conv = jnp.transpose(jnp.concatenate([nvx, nvb, nvc], axis=-1), (1, 0, 2))
    return y, hn.reshape(bd, SSM_HEADS, SSM_HEAD_DIM, ds), conv


def _group_lanes(v):
    v = v.reshape(SSM_GROUPS, 1, HEADS_PER_GROUP).astype(F32)
    return jnp.pad(v, ((0, 0), (0, 0), (0, LANES - HEADS_PER_GROUP)))


def kernel(x_prompt, x_sample, cache_kv_w128, cache_kv_w512, cache_kv_w2048, state_ssm, state_conv,
           attn_w_in, attn_w_out, ssm_w_in, ssm_conv_w, ssm_conv_b, ssm_dt_bias, ssm_a_log, ssm_d,
           ssm_norm_w, ssm_w_out, mlp_w1, mlp_w2, ln_mix_g, ln_mix_b, ln_ffn_g, ln_ffn_b):
    caches = (cache_kv_w128, cache_kv_w512, cache_kv_w2048)
    bp, seq, d = x_prompt.shape
    bd = x_sample.shape[0]
    mp = bp * seq
    xp = x_prompt.reshape(mp, d)
    xs = x_sample.reshape(bd, d)

    tm_p = 1024 if seq % 1024 == 0 else seq
    tm_qkv = 512 if seq % 512 == 0 else seq
    tm_ssm = 256 if seq % 256 == 0 else seq
    rope_p = _rope_tables(jnp.arange(seq, dtype=jnp.int32), SCALE * LOG2E)
    rope_s = _rope_tables(jnp.full((bd,), PAST_LEN, dtype=jnp.int32), SCALE)
    e4 = ((jnp.arange(GROUP_WIDTH) // SSM_HEAD_DIM)[None, :] == jnp.arange(LANES)[:, None]).astype(BF16)

    caches_t = [jnp.transpose(c, (0, 1, 3, 4, 5, 2)) for c in caches]
    updated = None
    kv_p = [[] for _ in ATTN_GROUPS]
    ssm_p, conv_p, ssm_s, conv_s = [], [], [], []

    for i in range(DEPTH):
        j = i // 2
        if i % 2 == 0:
            w_in = attn_w_in[j].astype(BF16)
            w_out = attn_w_out[j].astype(BF16)
            qkv_p = _project(xp, w_in, tm_qkv, QKV_WIDTH, rope=rope_p, table_blocks=seq // tm_qkv)
            qkv_s = _project(xs, w_in, bd, QKV_WIDTH, rope=rope_s)
            hp, *tails = _attention_prompt(qkv_p, bp, seq)
            hs, updated = _cache_attention(qkv_s, caches_t, updated, j)
            for gi in range(N_GROUPS):
                kv_p[gi].append(tails[gi])
        else:
            w_in = jnp.pad(ssm_w_in[j], ((0, 0), (0, SSM_IN_PAD - ssm_w_in.shape[2]))).astype(BF16)
            w_out = ssm_w_out[j].astype(BF16)
            prm = {
                "conv_w": ssm_conv_w[j],
                "conv_b": ssm_conv_b[j].reshape(1, CONV_DIM),
                "dt_bias_g": _group_lanes(ssm_dt_bias[j]),
                "a_log_g": _group_lanes(ssm_a_log[j]),
                "d_exp": jnp.repeat(ssm_d[j].astype(F32), SSM_HEAD_DIM).reshape(1, D_INNER),
                "norm_w": ssm_norm_w[j].reshape(1, D_INNER),
            }
            proj_p = _project(xp, w_in, tm_ssm, SSM_IN_PAD)
            proj_s = _project(xs, w_in, bd, SSM_IN_PAD)
            hp, st_p, cv_p = _ssd_prompt(proj_p, prm, bp, seq)
            hs, st_s, cv_s = _ssd_sample(proj_s, state_conv[j], state_ssm[j], prm, e4)
            ssm_p.append(st_p)
            conv_p.append(cv_p)
            ssm_s.append(st_s)
            conv_s.append(cv_s)
        xp = _out_ln(hp, w_out, xp, ln_mix_g[i], ln_mix_b[i], 512)
        xs = _out_ln(hs, w_out, xs, ln_mix_g[i], ln_mix_b[i], bd)
        w1 = mlp_w1[i].astype(BF16)
        w2 = mlp_w2[i].astype(BF16)
        xp = _mlp(xp, w1, w2, ln_ffn_g[i], ln_ffn_b[i], tm_p)
        xs = _mlp(xs, w1, w2, ln_ffn_g[i], ln_ffn_b[i], bd)

    back = (0, 1, 5, 2, 3, 4)
    kv_p_out = [jnp.transpose(jnp.stack(t), back) for t in kv_p]
    kv_s_out = [jnp.transpose(u, back) for u in updated]
    return (xp.reshape(bp, seq, d), xs.reshape(bd, 1, d),
            kv_p_out[0], kv_p_out[1], kv_p_out[2], jnp.stack(ssm_p), jnp.stack(conv_p),
            kv_s_out[0], kv_s_out[1], kv_s_out[2], jnp.stack(ssm_s), jnp.stack(conv_s))
```
